```python
import math
import jax
import jax.numpy as jnp
from jax import lax
import numpy as np


D_MODEL = 1024
BATCH = 2
SEQ = 16384
DEPTH = 2
DEC_BATCH = 8
DEC_SEQ = 2048
PAST_LEN = 128

HEAD_DIM = 64
SSM_HEADS = 16
SSM_HEAD_DIM = 64
SSM_INNER = SSM_HEADS * SSM_HEAD_DIM
SSM_GROUPS = 2
SSM_STATE = 128
SSM_CHUNK = 128
CONV_WIDTH = 5
CONV_CH = SSM_INNER + 2 * SSM_GROUPS * SSM_STATE
SWA_HEADS = 16
SWA_KV_HEADS = 4
SWA_WINDOW = 128
SWA_BLOCK = 128
DIL_PATTERNS = ((128, 1), (512, 4), (2048, 16))
DIL_HEADS_PER_GROUP = 4
DIL_HEADS = DIL_HEADS_PER_GROUP * 3
DIL_OUT = DIL_HEADS_PER_GROUP * HEAD_DIM
PEER_HEADS = 8
PEER_N_KEYS = 128
PEER_EXPERTS = PEER_N_KEYS * PEER_N_KEYS
PEER_TOPK = 16
PEER_QDIM = 256
PEER_TOKEN_BLOCK = 128
N_BRANCHES = 3
ROPE_THETA = 10000.0
EPS = 1e-6
NEG_INF = -1e30
IN_SPLITS = (SSM_INNER, SSM_INNER, SSM_GROUPS * SSM_STATE, SSM_GROUPS * SSM_STATE, SSM_HEADS, SSM_HEADS,
             SWA_HEADS * HEAD_DIM, SWA_KV_HEADS * HEAD_DIM, SWA_KV_HEADS * HEAD_DIM,
             DIL_HEADS * HEAD_DIM, DIL_HEADS * HEAD_DIM, DIL_HEADS * HEAD_DIM,
             N_BRANCHES * D_MODEL)
IN_COLS = sum(IN_SPLITS)

kernel_name = 'hybrid_ssd_swa_dilated_peer_encoder'


def rms_norm(x, w):
    xf = x.astype(jnp.float32)
    y = xf * lax.rsqrt(jnp.mean(xf * xf, axis=-1, keepdims=True) + EPS)
    return (y * w.astype(jnp.float32)).astype(x.dtype)


def rope(x, pos):
    half = x.shape[-1] // 2
    inv_freq = ROPE_THETA ** (-jnp.arange(half, dtype=jnp.float32) / half)
    ang = pos.astype(jnp.float32)[:, None] * inv_freq[None, :]
    cos = jnp.cos(ang)[None, :, None, :]
    sin = jnp.sin(ang)[None, :, None, :]
    xf = x.astype(jnp.float32)
    x1, x2 = xf[..., :half], xf[..., half:]
    return jnp.concatenate([x1 * cos - x2 * sin, x2 * cos + x1 * sin], axis=-1).astype(x.dtype)


def split_cols(t, sizes):
    out, start = [], 0
    for s in sizes:
        out.append(t[..., start:start + s])
        start += s
    return out


def banded_attention(q, k, v, window, block, sink):
    b, l, hkv, g, d = q.shape
    pad = (-l) % block
    n = (l + pad) // block
    qb = jnp.pad(q, ((0, 0), (0, pad), (0, 0), (0, 0), (0, 0))).reshape(b, n, block, hkv, g, d)
    kv_pad = ((0, 0), (block, block + pad), (0, 0), (0, 0))
    kb = jnp.pad(k, kv_pad).reshape(b, n + 2, block, hkv, d)
    vb = jnp.pad(v, kv_pad).reshape(b, n + 2, block, hkv, d)
    kw = jnp.concatenate([kb[:, :-2], kb[:, 1:-1], kb[:, 2:]], axis=2)
    vw = jnp.concatenate([vb[:, :-2], vb[:, 1:-1], vb[:, 2:]], axis=2)
    blk = jnp.arange(n)[:, None, None]
    qpos = blk * block + jnp.arange(block)[None, :, None]
    kpos = (blk - 1) * block + jnp.arange(3 * block)[None, None, :]
    mask = (jnp.abs(qpos - kpos) <= window) & (kpos >= 0) & (kpos < l)
    s = jnp.einsum('bnqhgd,bnkhd->bnhgqk', qb, kw, preferred_element_type=jnp.float32) * (d ** -0.5)
    s = jnp.where(mask[None, :, None, None], s, NEG_INF)
    m = jnp.max(s, axis=-1, keepdims=True)
    if sink is not None:
        sk = sink.astype(jnp.float32).reshape(1, 1, hkv, g, 1, 1)
        m = jnp.maximum(m, sk)
    p = jnp.exp(s - m)
    den = jnp.sum(p, axis=-1, keepdims=True)
    if sink is not None:
        den = den + jnp.exp(sk - m)
    o = jnp.einsum('bnhgqk,bnkhd->bnhgqd', p.astype(v.dtype), vw, preferred_element_type=jnp.float32) / den
    o = o.transpose(0, 1, 4, 2, 3, 5).reshape(b, n * block, hkv, g, d)[:, :l]
    lse = (m + jnp.log(den))[..., 0].transpose(0, 1, 4, 2, 3).reshape(b, n * block, hkv, g)[:, :l]
    return o.astype(q.dtype), lse


def ssd_scan(x, dt, a_neg, bm, cm):
    b, l, h, p = x.shape
    g, n = bm.shape[2], bm.shape[3]
    j = h // g
    cl = SSM_CHUNK
    c = l // cl
    f32 = jnp.float32
    xdt = (x.astype(f32) * dt[..., None]).reshape(b, c, cl, g, j, p)
    da = (dt * a_neg).reshape(b, c, cl, g, j)
    da_cs = jnp.cumsum(da, axis=2)
    bmc = bm.astype(f32).reshape(b, c, cl, g, n)
    cmc = cm.astype(f32).reshape(b, c, cl, g, n)
    cs_t = jnp.moveaxis(da_cs, 2, -1)
    diff = cs_t[..., :, None] - cs_t[..., None, :]
    tri = jnp.tril(jnp.ones((cl, cl), dtype=bool))
    decay_mat = jnp.where(tri, jnp.exp(jnp.where(tri, diff, 0.0)), 0.0)
    cb = jnp.einsum('bcqgn,bcsgn->bcgqs', cmc, bmc)
    y_diag = jnp.einsum('bcgjqs,bcsgjp->bcqgjp', cb[:, :, :, None] * decay_mat, xdt)
    decay_to_end = jnp.exp(da_cs[:, :, -1:] - da_cs)
    states = jnp.einsum('bcsgn,bcsgjp->bcgjpn', bmc, xdt * decay_to_end[..., None])
    chunk_decay = jnp.exp(da_cs[:, :, -1])

    def step(h_state, inp):
        st, dec = inp
        return h_state * dec[..., None, None] + st, h_state

    h0 = jnp.zeros((b, g, j, p, n), f32)
    _, h_in = lax.scan(step, h0, (jnp.moveaxis(states, 1, 0), jnp.moveaxis(chunk_decay, 1, 0)))
    h_in = jnp.moveaxis(h_in, 0, 1)
    y_off = jnp.einsum('bcqgn,bcgjpn->bcqgjp', cmc, h_in) * jnp.exp(da_cs)[..., None]
    return (y_diag + y_off).reshape(b, l, h, p)


def ssd_mixer(z, xs, bm, cm, dt_f, dt_b, conv_w, conv_b, dt_bias_f, dt_bias_b, a_log_f, a_log_b, d_skip, norm_w):
    b, l = xs.shape[:2]
    xbc = jnp.concatenate([xs, bm, cm], axis=-1)
    xbc = lax.conv_general_dilated(xbc, conv_w[:, None, :].astype(xbc.dtype), window_strides=(1,),
                                   padding=[(CONV_WIDTH // 2, CONV_WIDTH // 2)],
                                   dimension_numbers=('NWC', 'WIO', 'NWC'), feature_group_count=CONV_CH)
    xbc = jax.nn.silu(xbc + conv_b)
    xs, bm, cm = split_cols(xbc, (SSM_INNER, SSM_GROUPS * SSM_STATE, SSM_GROUPS * SSM_STATE))
    xs = xs.reshape(b, l, SSM_HEADS, SSM_HEAD_DIM)
    bm = bm.reshape(b, l, SSM_GROUPS, SSM_STATE)
    cm = cm.reshape(b, l, SSM_GROUPS, SSM_STATE)
    dtf = jax.nn.softplus(dt_f.astype(jnp.float32) + dt_bias_f)
    dtb = jax.nn.softplus(dt_b.astype(jnp.float32) + dt_bias_b)
    y_fwd = ssd_scan(xs, dtf, -jnp.exp(a_log_f.astype(jnp.float32)), bm, cm)

    def flip(t):
        return jnp.flip(t, axis=1)

    y_bwd = flip(ssd_scan(flip(xs), flip(dtb), -jnp.exp(a_log_b.astype(jnp.float32)), flip(bm), flip(cm)))
    y = y_fwd + y_bwd + d_skip[:, None] * xs.astype(jnp.float32)
    y = y.reshape(b, l, SSM_INNER) * jax.nn.silu(z.astype(jnp.float32))
    return rms_norm(y, norm_w).astype(z.dtype)


def window_gqa(q, k, v, q_gain, k_gain, sink, pos):
    b, l = q.shape[:2]
    grp = SWA_HEADS // SWA_KV_HEADS
    q = rope(rms_norm(q.reshape(b, l, SWA_HEADS, HEAD_DIM), q_gain), pos)
    k = rope(rms_norm(k.reshape(b, l, SWA_KV_HEADS, HEAD_DIM), k_gain), pos)
    v = v.reshape(b, l, SWA_KV_HEADS, HEAD_DIM)
    o, _ = banded_attention(q.reshape(b, l, SWA_KV_HEADS, grp, HEAD_DIM), k, v, SWA_WINDOW, SWA_BLOCK,
                            sink.reshape(SWA_KV_HEADS, grp))
    return o.reshape(b, l, SWA_HEADS * HEAD_DIM)


def dilated_attention(q, k, v, q_gain, k_gain, pos):
    b, l = q.shape[:2]
    nh = DIL_HEADS_PER_GROUP
    q = rope(rms_norm(q.reshape(b, l, DIL_HEADS, HEAD_DIM), q_gain), pos)
    k = rope(rms_norm(k.reshape(b, l, DIL_HEADS, HEAD_DIM), k_gain), pos)
    v = v.reshape(b, l, DIL_HEADS, HEAD_DIM)
    outs, lses = [], []
    for gi, (win, dil) in enumerate(DIL_PATTERNS):
        sl = slice(gi * nh, (gi + 1) * nh)
        ls = l // dil

        def gather(t):
            return t[:, :, sl].reshape(b, ls, dil, nh, HEAD_DIM).transpose(0, 2, 1, 3, 4).reshape(b * dil, ls, nh, HEAD_DIM)

        half = win // (2 * dil)
        o, lse = banded_attention(gather(q)[:, :, :, None], gather(k), gather(v), half, half, None)
        outs.append(o[:, :, :, 0].reshape(b, dil, ls, nh, HEAD_DIM).transpose(0, 2, 1, 3, 4).reshape(b, l, nh, HEAD_DIM))
        lses.append(lse[..., 0].reshape(b, dil, ls, nh).transpose(0, 2, 1, 3).reshape(b, l, nh))
    wts = jax.nn.softmax(jnp.stack(lses, axis=0), axis=0)
    o = jnp.sum(wts[..., None] * jnp.stack(outs, axis=0).astype(jnp.float32), axis=0)
    return o.reshape(b, l, DIL_OUT).astype(q.dtype)


def peer_ffn(h, wq, k1, k2, u, v):
    b, l, dm = h.shape
    t_count = b * l
    t = h.reshape(t_count, dm)
    q = (t @ wq).astype(jnp.float32).reshape(t_count, PEER_HEADS, 2, PEER_QDIM // 2)
    s1 = jnp.einsum('thd,kd->thk', q[:, :, 0], k1.astype(jnp.float32))
    s2 = jnp.einsum('thd,kd->thk', q[:, :, 1], k2.astype(jnp.float32))
    v1, i1 = lax.top_k(s1, PEER_TOPK)
    v2, i2 = lax.top_k(s2, PEER_TOPK)
    cand_s = (v1[..., :, None] + v2[..., None, :]).reshape(t_count, PEER_HEADS, PEER_TOPK * PEER_TOPK)
    cand_i = (i1[..., :, None] * PEER_N_KEYS + i2[..., None, :]).reshape(t_count, PEER_HEADS, PEER_TOPK * PEER_TOPK)
    top_s, top_pos = lax.top_k(cand_s, PEER_TOPK)
    idx = jnp.take_along_axis(cand_i, top_pos, axis=-1)
    gate = jax.nn.softmax(top_s, axis=-1)
    nblk = t_count // PEER_TOKEN_BLOCK

    def expert_block(args):
        tb, ib, gb = args
        ue = jnp.take(u, ib, axis=0)
        act = jax.nn.gelu(jnp.einsum('td,thkd->thk', tb, ue, preferred_element_type=jnp.float32), approximate=False)
        ve = jnp.take(v, ib, axis=0)
        return jnp.einsum('thk,thkd->td', (gb * act).astype(v.dtype), ve, preferred_element_type=jnp.float32)

    out = lax.map(expert_block, (t.reshape(nblk, PEER_TOKEN_BLOCK, dm),
                                 idx.reshape(nblk, PEER_TOKEN_BLOCK, PEER_HEADS, PEER_TOPK),
                                 gate.reshape(nblk, PEER_TOKEN_BLOCK, PEER_HEADS, PEER_TOPK)))
    return out.reshape(b, l, dm).astype(h.dtype)


def encoder_layer(x, norm1_w, w_in, conv_w, conv_b, dt_bias_f, dt_bias_b, a_log_f, a_log_b, d_skip, ssm_norm_w,
                  w_a, q_norm_b, k_norm_b, sink_b, w_b, q_norm_c, k_norm_c, w_c, w_out, norm2_w,
                  peer_wq, peer_k1, peer_k2, peer_u, peer_v):
    b, l, _ = x.shape
    pos = jnp.arange(l)
    h = rms_norm(x, norm1_w)
    z, xs, bm, cm, dt_f, dt_b, q_b, k_b, v_b, q_c, k_c, v_c, gate_logits = split_cols(h @ w_in, IN_SPLITS)
    y_a = ssd_mixer(z, xs, bm, cm, dt_f, dt_b, conv_w, conv_b, dt_bias_f, dt_bias_b, a_log_f, a_log_b,
                    d_skip, ssm_norm_w) @ w_a
    y_b = window_gqa(q_b, k_b, v_b, q_norm_b, k_norm_b, sink_b, pos) @ w_b
    y_c = dilated_attention(q_c, k_c, v_c, q_norm_c, k_norm_c, pos) @ w_c
    gates = jax.nn.sigmoid(gate_logits.astype(jnp.float32)).reshape(b, l, N_BRANCHES, D_MODEL)
    merged = gates[:, :, 0] * y_a + gates[:, :, 1] * y_b + gates[:, :, 2] * y_c
    x = x + (merged.astype(x.dtype) @ w_out).astype(x.dtype)
    x = x + peer_ffn(rms_norm(x, norm2_w), peer_wq, peer_k1, peer_k2, peer_u, peer_v)
    return x


def _dt_bias(k, shape):
    dt = jnp.exp(jax.random.uniform(k, shape, jnp.float32, math.log(1e-3), math.log(1e-1)))
    return dt + jnp.log(-jnp.expm1(-dt))


def setup_inputs(seed: int = 0) -> dict:
    key = jax.random.key(seed)
    ks = jax.random.split(key, 27)
    f32 = jnp.float32

    def nrm(k, shape, scale):
        return jax.random.normal(k, shape, f32) * scale

    def gain(k, shape):
        return 1.0 + nrm(k, shape, 0.05)

    L = DEPTH
    return {
        'x_prompt': nrm(ks[0], (BATCH, SEQ, D_MODEL), 1.0),
        'x_sample': nrm(ks[1], (DEC_BATCH, DEC_SEQ, D_MODEL), 1.0),
        'norm1_w': gain(ks[2], (L, D_MODEL)),
        'w_in': nrm(ks[3], (L, D_MODEL, IN_COLS), D_MODEL ** -0.5),
        'conv_w': nrm(ks[4], (L, CONV_WIDTH, CONV_CH), CONV_WIDTH ** -0.5),
        'conv_b': nrm(ks[5], (L, CONV_CH), 0.02),
        'dt_bias_f': _dt_bias(ks[6], (L, SSM_HEADS)),
        'dt_bias_b': _dt_bias(ks[7], (L, SSM_HEADS)),
        'a_log_f': jnp.log(jax.random.uniform(ks[8], (L, SSM_HEADS), f32, 1.0, 16.0)),
        'a_log_b': jnp.log(jax.random.uniform(ks[9], (L, SSM_HEADS), f32, 1.0, 16.0)),
        'd_skip': 1.0 + nrm(ks[10], (L, SSM_HEADS), 0.1),
        'ssm_norm_w': gain(ks[11], (L, SSM_INNER)),
        'w_a': nrm(ks[12], (L, SSM_INNER, D_MODEL), SSM_INNER ** -0.5),
        'q_norm_b': gain(ks[13], (L, HEAD_DIM)),
        'k_norm_b': gain(ks[14], (L, HEAD_DIM)),
        'sink_b': nrm(ks[15], (L, SWA_HEADS), 0.5),
        'w_b': nrm(ks[16], (L, SWA_HEADS * HEAD_DIM, D_MODEL), (SWA_HEADS * HEAD_DIM) ** -0.5),
        'q_norm_c': gain(ks[17], (L, HEAD_DIM)),
        'k_norm_c': gain(ks[18], (L, HEAD_DIM)),
        'w_c': nrm(ks[19], (L, DIL_OUT, D_MODEL), DIL_OUT ** -0.5),
        'w_out': nrm(ks[20], (L, D_MODEL, D_MODEL), D_MODEL ** -0.5),
        'norm2_w': gain(ks[21], (L, D_MODEL)),
        'peer_wq': nrm(ks[22], (L, D_MODEL, PEER_HEADS * PEER_QDIM), D_MODEL ** -0.5),
        'peer_k1': nrm(ks[23], (L, PEER_N_KEYS, PEER_QDIM // 2), (PEER_QDIM // 2) ** -0.5),
        'peer_k2': nrm(ks[24], (L, PEER_N_KEYS, PEER_QDIM // 2), (PEER_QDIM // 2) ** -0.5),
        'peer_u': nrm(ks[25], (L, PEER_EXPERTS, D_MODEL), D_MODEL ** -0.5),
        'peer_v': nrm(ks[26], (L, PEER_EXPERTS, D_MODEL), 0.3),
    }


def reference(x_prompt, x_sample, norm1_w, w_in, conv_w, conv_b, dt_bias_f, dt_bias_b, a_log_f, a_log_b, d_skip,
              ssm_norm_w, w_a, q_norm_b, k_norm_b, sink_b, w_b, q_norm_c, k_norm_c, w_c, w_out, norm2_w,
              peer_wq, peer_k1, peer_k2, peer_u, peer_v):
    def trunk(x):
        for i in range(DEPTH):
            x = encoder_layer(x, norm1_w[i], w_in[i], conv_w[i], conv_b[i], dt_bias_f[i], dt_bias_b[i],
                              a_log_f[i], a_log_b[i], d_skip[i], ssm_norm_w[i], w_a[i], q_norm_b[i], k_norm_b[i],
                              sink_b[i], w_b[i], q_norm_c[i], k_norm_c[i], w_c[i], w_out[i], norm2_w[i],
                              peer_wq[i], peer_k1[i], peer_k2[i], peer_u[i], peer_v[i])
        return x

    y_prompt = trunk(x_prompt)
    y_sample = trunk(x_sample)
    return (y_prompt, y_sample)
```

```python
import functools
import math

import jax
import jax.numpy as jnp
from jax import lax
from jax.experimental import pallas as pl
from jax.experimental.pallas import tpu as pltpu

D_MODEL = 1024
HEAD_DIM = 64
SSM_HEADS = 16
SSM_HEAD_DIM = 64
SSM_INNER = SSM_HEADS * SSM_HEAD_DIM
SSM_GROUPS = 2
SSM_STATE = 128
SSM_CHUNK = 128
CONV_WIDTH = 5
CONV_CH = SSM_INNER + 2 * SSM_GROUPS * SSM_STATE
SWA_HEADS = 16
SWA_KV_HEADS = 4
SWA_WINDOW = 128
SWA_BLOCK = 128
DIL_PATTERNS = ((128, 1), (512, 4), (2048, 16))
DIL_HEADS_PER_GROUP = 4
DIL_HEADS = DIL_HEADS_PER_GROUP * 3
DIL_OUT = DIL_HEADS_PER_GROUP * HEAD_DIM
PEER_HEADS = 8
PEER_N_KEYS = 128
PEER_EXPERTS = PEER_N_KEYS * PEER_N_KEYS
PEER_TOPK = 16
PEER_QDIM = 256
PEER_PICKS = PEER_HEADS * PEER_TOPK
N_BRANCHES = 3
ROPE_THETA = 10000.0
EPS = 1e-6
NEG_INF = -1e30
IN_SPLITS = (SSM_INNER, SSM_INNER, SSM_GROUPS * SSM_STATE, SSM_GROUPS * SSM_STATE, SSM_HEADS, SSM_HEADS,
             SWA_HEADS * HEAD_DIM, SWA_KV_HEADS * HEAD_DIM, SWA_KV_HEADS * HEAD_DIM,
             DIL_HEADS * HEAD_DIM, DIL_HEADS * HEAD_DIM, DIL_HEADS * HEAD_DIM,
             N_BRANCHES * D_MODEL)
IN_COLS = sum(IN_SPLITS)

LANES = 128
SUBLANES = 8
VMEM_LIMIT_BYTES = 56 * 1024 * 1024

PACK_WORDS = D_MODEL // 2
PACK_ROWS = PACK_WORDS // LANES
ROW_SUBLANES = D_MODEL // LANES
PEER_TB = 32


def _mm_kernel(x_ref, g_ref, w_ref, o_ref, xn_ref, *, norm):
    @pl.when(pl.program_id(1) == 0)
    def _():
        x = x_ref[...].astype(jnp.float32)
        if norm:
            x = x * lax.rsqrt(jnp.mean(x * x, axis=-1, keepdims=True) + EPS) * g_ref[...]
        xn_ref[...] = x.astype(jnp.bfloat16)

    o_ref[...] = jnp.dot(xn_ref[...], w_ref[...], preferred_element_type=jnp.float32)


def _matmul(x, w, gain=None, tm=1024, tn=512):
    m, k = x.shape
    n = w.shape[1]
    tm = min(tm, m)
    tn = min(tn, n)
    assert m % tm == 0
    n_pad = -n % tn
    wb = w.astype(jnp.bfloat16)
    if n_pad:
        wb = jnp.pad(wb, ((0, 0), (0, n_pad)))
    g = jnp.ones((1, k), jnp.float32) if gain is None else gain.reshape(1, k).astype(jnp.float32)
    out = pl.pallas_call(
        functools.partial(_mm_kernel, norm=gain is not None),
        grid=(m // tm, (n + n_pad) // tn),
        in_specs=[pl.BlockSpec((tm, k), lambda i, j: (i, 0)),
                  pl.BlockSpec((1, k), lambda i, j: (0, 0)),
                  pl.BlockSpec((k, tn), lambda i, j: (0, j))],
        out_specs=pl.BlockSpec((tm, tn), lambda i, j: (i, j)),
        out_shape=jax.ShapeDtypeStruct((m, n + n_pad), jnp.float32),
        scratch_shapes=[pltpu.VMEM((tm, k), jnp.bfloat16)],
        compiler_params=pltpu.CompilerParams(dimension_semantics=("parallel", "arbitrary"),
                                             vmem_limit_bytes=VMEM_LIMIT_BYTES),
        name="proj_matmul",
    )(x, g, wb)
    return out[:, :n] if n_pad else out


def _pack_table(tab):
    e = tab.shape[0]
    tb = lax.bitcast_convert_type(tab.astype(jnp.bfloat16), jnp.uint16).astype(jnp.uint32)
    tb = tb.reshape(e, PACK_ROWS, 2, LANES)
    packed = tb[:, :, 0, :] | (tb[:, :, 1, :] << 16)
    return lax.bitcast_convert_type(packed, jnp.int32).reshape(e * PACK_ROWS, LANES)


def _peer_constants():
    col = jnp.arange(PEER_PICKS * ROW_SUBLANES)
    diag = (col[None, :] % ROW_SUBLANES == jnp.arange(ROW_SUBLANES)[:, None]).astype(jnp.float32)
    group = (col[:, None] // ROW_SUBLANES == jnp.arange(PEER_PICKS)[None, :]).astype(jnp.bfloat16)
    rep = (jnp.arange(PEER_TB * ROW_SUBLANES)[:, None] // ROW_SUBLANES == jnp.arange(PEER_TB)[None, :])
    return diag, group, group.T, rep.astype(jnp.bfloat16)


def _split3(x):
    hi = x.astype(jnp.bfloat16).astype(jnp.float32)
    r1 = x - hi
    mid = r1.astype(jnp.bfloat16).astype(jnp.float32)
    return hi, mid, r1 - mid


def _stack_pieces(pieces):
    return jnp.concatenate(list(pieces) + [jnp.zeros_like(pieces[0])], axis=1).astype(jnp.bfloat16)


def _sum_pieces(y):
    s = ROW_SUBLANES
    return y[0:s] + y[s:2 * s] + y[2 * s:3 * s]


def _gather_rows(idx_ref, tab_ref, t, raw_ref):
    for j in range(PEER_PICKS):
        e = idx_ref[t, j]
        src = pl.ds(pl.multiple_of(e * PACK_ROWS, PACK_ROWS), PACK_ROWS)
        raw_ref[pl.ds(j * PACK_ROWS, PACK_ROWS), :] = tab_ref[src, :]


def _token_rows(t):
    return pl.ds(pl.multiple_of(t * ROW_SUBLANES, ROW_SUBLANES), ROW_SUBLANES)


def _pipelined_tokens(gather, compute, bufs):
    gather(0, bufs[0])

    def step(t, carry):
        nxt = jnp.minimum(t + 1, PEER_TB - 1)
        for parity in range(2):
            @pl.when(t % 2 == parity)
            def _():
                compute(t, bufs[parity])
                gather(nxt, bufs[1 - parity])
        return carry

    lax.fori_loop(0, PEER_TB, step, 0)


def _peer_act_kernel(idx_ref, x_ref, gate_ref, diag_ref, group_ref, tab_ref, w_ref, raw0_ref, raw1_ref, x3_ref,
                     picked_ref):
    nt = (((1,), (1,)), ((), ()))
    x3_ref[...] = _stack_pieces(_split3(x_ref[...].reshape(PEER_TB, ROW_SUBLANES, LANES)))

    def compute(t, buf_ref):
        wmat = pltpu.bitcast(buf_ref[...], jnp.bfloat16)
        scores = _sum_pieces(lax.dot_general(x3_ref[t], wmat, nt, preferred_element_type=jnp.float32))
        picked_ref[_token_rows(t), :] = scores * diag_ref[...]

    _pipelined_tokens(functools.partial(_gather_rows, idx_ref, tab_ref), compute, (raw0_ref, raw1_ref))

    act8 = sum(jnp.dot(piece.astype(jnp.bfloat16), group_ref[...], preferred_element_type=jnp.float32)
               for piece in _split3(picked_ref[...]))
    act = jnp.sum(act8.reshape(PEER_TB, ROW_SUBLANES, PEER_PICKS), axis=1)
    gelu = 0.5 * act * (1.0 + lax.erf(act * (2.0 ** -0.5)))
    w_ref[...] = gate_ref[...] * gelu


def _peer_out_kernel(idx_ref, w_ref, res_ref, diag_ref, expand_ref, rep_ref, tab_ref, o_ref, raw0_ref, raw1_ref,
                     lhs_ref):
    pieces = []
    for piece in _split3(w_ref[...]):
        rep = jnp.dot(rep_ref[...], piece.astype(jnp.bfloat16), preferred_element_type=jnp.float32)
        wide = jnp.dot(rep.astype(jnp.bfloat16), expand_ref[...], preferred_element_type=jnp.float32)
        pieces.append(wide.reshape(PEER_TB, ROW_SUBLANES, PEER_PICKS * ROW_SUBLANES) * diag_ref[...][None])
    lhs_ref[...] = _stack_pieces(pieces)

    def compute(t, buf_ref):
        wmat = pltpu.bitcast(buf_ref[...], jnp.bfloat16)
        rows = _token_rows(t)
        o_ref[rows, :] = res_ref[rows, :] + _sum_pieces(jnp.dot(lhs_ref[t], wmat, preferred_element_type=jnp.float32))

    _pipelined_tokens(functools.partial(_gather_rows, idx_ref, tab_ref), compute, (raw0_ref, raw1_ref))


def _const_spec(shape):
    return pl.BlockSpec(shape, lambda i: (0,) * len(shape), pipeline_mode=pl.Buffered(1))


def _peer_experts(x_res, xn, idx, gate, u_packed, v_packed):
    t = xn.shape[0]
    assert t % PEER_TB == 0 and PEER_TB % 2 == 0
    grid = (t // PEER_TB,)
    params = pltpu.CompilerParams(dimension_semantics=("arbitrary",), vmem_limit_bytes=VMEM_LIMIT_BYTES)
    smem_spec = pl.BlockSpec((PEER_TB, PEER_PICKS), lambda i: (i, 0), memory_space=pltpu.SMEM)
    rows_spec = pl.BlockSpec((PEER_TB * ROW_SUBLANES, LANES), lambda i: (i, 0))
    picks_spec = pl.BlockSpec((PEER_TB, PEER_PICKS), lambda i: (i, 0))
    raw = pltpu.VMEM((PEER_PICKS * PACK_ROWS, LANES), jnp.int32)
    diag, group, expand, rep = _peer_constants()
    stacked = 4 * ROW_SUBLANES

    w = pl.pallas_call(
        _peer_act_kernel,
        grid=grid,
        in_specs=[smem_spec, rows_spec, picks_spec, _const_spec(diag.shape), _const_spec(group.shape),
                  _const_spec(u_packed.shape)],
        out_specs=picks_spec,
        out_shape=jax.ShapeDtypeStruct((t, PEER_PICKS), jnp.float32),
        scratch_shapes=[raw, raw, pltpu.VMEM((PEER_TB, stacked, LANES), jnp.bfloat16),
                        pltpu.VMEM((PEER_TB * ROW_SUBLANES, PEER_PICKS * ROW_SUBLANES), jnp.float32)],
        compiler_params=params,
        name="peer_act",
    )(idx, xn.reshape(t * ROW_SUBLANES, LANES), gate, diag, group, u_packed)

    out = pl.pallas_call(
        _peer_out_kernel,
        grid=grid,
        in_specs=[smem_spec, picks_spec, rows_spec, _const_spec(diag.shape), _const_spec(expand.shape),
                  _const_spec(rep.shape), _const_spec(v_packed.shape)],
        out_specs=rows_spec,
        out_shape=jax.ShapeDtypeStruct((t * ROW_SUBLANES, LANES), jnp.float32),
        scratch_shapes=[raw, raw, pltpu.VMEM((PEER_TB, stacked, PEER_PICKS * ROW_SUBLANES), jnp.bfloat16)],
        compiler_params=params,
        name="peer_out",
    )(idx, w, x_res.reshape(t * ROW_SUBLANES, LANES), diag, expand, rep, v_packed)
    return out.reshape(t, D_MODEL)


def rms_norm(x, w):
    xf = x.astype(jnp.float32)
    y = xf * lax.rsqrt(jnp.mean(xf * xf, axis=-1, keepdims=True) + EPS)
    return (y * w.astype(jnp.float32)).astype(x.dtype)


def rope(x, pos):
    half = x.shape[-1] // 2
    inv_freq = ROPE_THETA ** (-jnp.arange(half, dtype=jnp.float32) / half)
    ang = pos.astype(jnp.float32)[:, None] * inv_freq[None, :]
    cos = jnp.cos(ang)[None, :, None, :]
    sin = jnp.sin(ang)[None, :, None, :]
    xf = x.astype(jnp.float32)
    x1, x2 = xf[..., :half], xf[..., half:]
    return jnp.concatenate([x1 * cos - x2 * sin, x2 * cos + x1 * sin], axis=-1).astype(x.dtype)


def split_cols(t, sizes):
    out, start = [], 0
    for s in sizes:
        out.append(t[..., start:start + s])
        start += s
    return out


def banded_attention(q, k, v, window, block, sink):
    b, l, hkv, g, d = q.shape
    pad = (-l) % block
    n = (l + pad) // block
    qb = jnp.pad(q, ((0, 0), (0, pad), (0, 0), (0, 0), (0, 0))).reshape(b, n, block, hkv, g, d)
    kv_pad = ((0, 0), (block, block + pad), (0, 0), (0, 0))
    kb = jnp.pad(k, kv_pad).reshape(b, n + 2, block, hkv, d)
    vb = jnp.pad(v, kv_pad).reshape(b, n + 2, block, hkv, d)
    kw = jnp.concatenate([kb[:, :-2], kb[:, 1:-1], kb[:, 2:]], axis=2)
    vw = jnp.concatenate([vb[:, :-2], vb[:, 1:-1], vb[:, 2:]], axis=2)
    blk = jnp.arange(n)[:, None, None]
    qpos = blk * block + jnp.arange(block)[None, :, None]
    kpos = (blk - 1) * block + jnp.arange(3 * block)[None, None, :]
    mask = (jnp.abs(qpos - kpos) <= window) & (kpos >= 0) & (kpos < l)
    s = jnp.einsum('bnqhgd,bnkhd->bnhgqk', qb, kw, preferred_element_type=jnp.float32) * (d ** -0.5)
    s = jnp.where(mask[None, :, None, None], s, NEG_INF)
    m = jnp.max(s, axis=-1, keepdims=True)
    if sink is not None:
        sk = sink.astype(jnp.float32).reshape(1, 1, hkv, g, 1, 1)
        m = jnp.maximum(m, sk)
    p = jnp.exp(s - m)
    den = jnp.sum(p, axis=-1, keepdims=True)
    if sink is not None:
        den = den + jnp.exp(sk - m)
    o = jnp.einsum('bnhgqk,bnkhd->bnhgqd', p.astype(v.dtype), vw, preferred_element_type=jnp.float32) / den
    o = o.transpose(0, 1, 4, 2, 3, 5).reshape(b, n * block, hkv, g, d)[:, :l]
    lse = (m + jnp.log(den))[..., 0].transpose(0, 1, 4, 2, 3).reshape(b, n * block, hkv, g)[:, :l]
    return o.astype(q.dtype), lse


def ssd_scan(x, dt, a_neg, bm, cm):
    b, l, h, p = x.shape
    g, n = bm.shape[2], bm.shape[3]
    j = h // g
    cl = SSM_CHUNK
    c = l // cl
    f32 = jnp.float32
    xdt = (x.astype(f32) * dt[..., None]).reshape(b, c, cl, g, j, p)
    da = (dt * a_neg).reshape(b, c, cl, g, j)
    da_cs = jnp.cumsum(da, axis=2)
    bmc = bm.astype(f32).reshape(b, c, cl, g, n)
    cmc = cm.astype(f32).reshape(b, c, cl, g, n)
    cs_t = jnp.moveaxis(da_cs, 2, -1)
    diff = cs_t[..., :, None] - cs_t[..., None, :]
    tri = jnp.tril(jnp.ones((cl, cl), dtype=bool))
    decay_mat = jnp.where(tri, jnp.exp(jnp.where(tri, diff, 0.0)), 0.0)
    cb = jnp.einsum('bcqgn,bcsgn->bcgqs', cmc, bmc)
    y_diag = jnp.einsum('bcgjqs,bcsgjp->bcqgjp', cb[:, :, :, None] * decay_mat, xdt)
    decay_to_end = jnp.exp(da_cs[:, :, -1:] - da_cs)
    states = jnp.einsum('bcsgn,bcsgjp->bcgjpn', bmc, xdt * decay_to_end[..., None])
    chunk_decay = jnp.exp(da_cs[:, :, -1])

    def step(h_state, inp):
        st, dec = inp
        return h_state * dec[..., None, None] + st, h_state

    h0 = jnp.zeros((b, g, j, p, n), f32)
    _, h_in = lax.scan(step, h0, (jnp.moveaxis(states, 1, 0), jnp.moveaxis(chunk_decay, 1, 0)))
    h_in = jnp.moveaxis(h_in, 0, 1)
    y_off = jnp.einsum('bcqgn,bcgjpn->bcqgjp', cmc, h_in) * jnp.exp(da_cs)[..., None]
    return (y_diag + y_off).reshape(b, l, h, p)


def ssd_mixer(z, xs, bm, cm, dt_f, dt_b, conv_w, conv_b, dt_bias_f, dt_bias_b, a_log_f, a_log_b, d_skip, norm_w):
    b, l = xs.shape[:2]
    xbc = jnp.concatenate([xs, bm, cm], axis=-1)
    xbc = lax.conv_general_dilated(xbc, conv_w[:, None, :].astype(xbc.dtype), window_strides=(1,),
                                   padding=[(CONV_WIDTH // 2, CONV_WIDTH // 2)],
                                   dimension_numbers=('NWC', 'WIO', 'NWC'), feature_group_count=CONV_CH)
    xbc = jax.nn.silu(xbc + conv_b)
    xs, bm, cm = split_cols(xbc, (SSM_INNER, SSM_GROUPS * SSM_STATE, SSM_GROUPS * SSM_STATE))
    xs = xs.reshape(b, l, SSM_HEADS, SSM_HEAD_DIM)
    bm = bm.reshape(b, l, SSM_GROUPS, SSM_STATE)
    cm = cm.reshape(b, l, SSM_GROUPS, SSM_STATE)
    dtf = jax.nn.softplus(dt_f.astype(jnp.float32) + dt_bias_f)
    dtb = jax.nn.softplus(dt_b.astype(jnp.float32) + dt_bias_b)
    y_fwd = ssd_scan(xs, dtf, -jnp.exp(a_log_f.astype(jnp.float32)), bm, cm)

    def flip(t):
        return jnp.flip(t, axis=1)

    y_bwd = flip(ssd_scan(flip(xs), flip(dtb), -jnp.exp(a_log_b.astype(jnp.float32)), flip(bm), flip(cm)))
    y = y_fwd + y_bwd + d_skip[:, None] * xs.astype(jnp.float32)
    y = y.reshape(b, l, SSM_INNER) * jax.nn.silu(z.astype(jnp.float32))
    return rms_norm(y, norm_w).astype(z.dtype)


def window_gqa(q, k, v, q_gain, k_gain, sink, pos):
    b, l = q.shape[:2]
    grp = SWA_HEADS // SWA_KV_HEADS
    q = rope(rms_norm(q.reshape(b, l, SWA_HEADS, HEAD_DIM), q_gain), pos)
    k = rope(rms_norm(k.reshape(b, l, SWA_KV_HEADS, HEAD_DIM), k_gain), pos)
    v = v.reshape(b, l, SWA_KV_HEADS, HEAD_DIM)
    o, _ = banded_attention(q.reshape(b, l, SWA_KV_HEADS, grp, HEAD_DIM), k, v, SWA_WINDOW, SWA_BLOCK,
                            sink.reshape(SWA_KV_HEADS, grp))
    return o.reshape(b, l, SWA_HEADS * HEAD_DIM)


def dilated_attention(q, k, v, q_gain, k_gain, pos):
    b, l = q.shape[:2]
    nh = DIL_HEADS_PER_GROUP
    q = rope(rms_norm(q.reshape(b, l, DIL_HEADS, HEAD_DIM), q_gain), pos)
    k = rope(rms_norm(k.reshape(b, l, DIL_HEADS, HEAD_DIM), k_gain), pos)
    v = v.reshape(b, l, DIL_HEADS, HEAD_DIM)
    outs, lses = [], []
    for gi, (win, dil) in enumerate(DIL_PATTERNS):
        sl = slice(gi * nh, (gi + 1) * nh)
        ls = l // dil

        def gather(t):
            return t[:, :, sl].reshape(b, ls, dil, nh, HEAD_DIM).transpose(0, 2, 1, 3, 4).reshape(b * dil, ls, nh, HEAD_DIM)

        half = win // (2 * dil)
        o, lse = banded_attention(gather(q)[:, :, :, None], gather(k), gather(v), half, half, None)
        outs.append(o[:, :, :, 0].reshape(b, dil, ls, nh, HEAD_DIM).transpose(0, 2, 1, 3, 4).reshape(b, l, nh, HEAD_DIM))
        lses.append(lse[..., 0].reshape(b, dil, ls, nh).transpose(0, 2, 1, 3).reshape(b, l, nh))
    wts = jax.nn.softmax(jnp.stack(lses, axis=0), axis=0)
    o = jnp.sum(wts[..., None] * jnp.stack(outs, axis=0).astype(jnp.float32), axis=0)
    return o.reshape(b, l, DIL_OUT).astype(q.dtype)


def _peer_route(q, k1, k2):
    t_count = q.shape[0]
    q = q.reshape(t_count, PEER_HEADS, 2, PEER_QDIM // 2)
    s1 = jnp.einsum('thd,kd->thk', q[:, :, 0], k1.astype(jnp.float32))
    s2 = jnp.einsum('thd,kd->thk', q[:, :, 1], k2.astype(jnp.float32))
    v1, i1 = lax.top_k(s1, PEER_TOPK)
    v2, i2 = lax.top_k(s2, PEER_TOPK)
    cand_s = (v1[..., :, None] + v2[..., None, :]).reshape(t_count, PEER_HEADS, PEER_TOPK * PEER_TOPK)
    cand_i = (i1[..., :, None] * PEER_N_KEYS + i2[..., None, :]).reshape(t_count, PEER_HEADS, PEER_TOPK * PEER_TOPK)
    top_s, top_pos = lax.top_k(cand_s, PEER_TOPK)
    idx = jnp.take_along_axis(cand_i, top_pos, axis=-1)
    gate = jax.nn.softmax(top_s, axis=-1)
    return idx.reshape(t_count, PEER_PICKS).astype(jnp.int32), gate.reshape(t_count, PEER_PICKS)


def _encoder_layer(x, p):
    b, l, _ = x.shape
    t = b * l
    pos = jnp.arange(l)
    xt = x.reshape(t, D_MODEL)
    proj = _matmul(xt, p['w_in'], gain=p['norm1_w']).reshape(b, l, IN_COLS)
    z, xs, bm, cm, dt_f, dt_b, q_b, k_b, v_b, q_c, k_c, v_c, gate_logits = split_cols(proj, IN_SPLITS)
    y_a = ssd_mixer(z, xs, bm, cm, dt_f, dt_b, p['conv_w'], p['conv_b'], p['dt_bias_f'], p['dt_bias_b'],
                    p['a_log_f'], p['a_log_b'], p['d_skip'], p['ssm_norm_w'])
    y_a = _matmul(y_a.reshape(t, SSM_INNER), p['w_a'])
    y_b = _matmul(window_gqa(q_b, k_b, v_b, p['q_norm_b'], p['k_norm_b'], p['sink_b'], pos).reshape(t, -1), p['w_b'])
    y_c = _matmul(dilated_attention(q_c, k_c, v_c, p['q_norm_c'], p['k_norm_c'], pos).reshape(t, -1), p['w_c'])
    gates = jax.nn.sigmoid(gate_logits.astype(jnp.float32)).reshape(t, N_BRANCHES, D_MODEL)
    merged = gates[:, 0] * y_a + gates[:, 1] * y_b + gates[:, 2] * y_c
    xt = xt + _matmul(merged, p['w_out'])
    q = _matmul(xt, p['peer_wq'], gain=p['norm2_w'])
    idx, gate = _peer_route(q, p['peer_k1'], p['peer_k2'])
    xn = rms_norm(xt, p['norm2_w'])
    xt = _peer_experts(xt, xn, idx, gate, p['u_packed'], p['v_packed'])
    return xt.reshape(b, l, D_MODEL)


_LAYER_KEYS = ('norm1_w', 'w_in', 'conv_w', 'conv_b', 'dt_bias_f', 'dt_bias_b', 'a_log_f', 'a_log_b', 'd_skip',
               'ssm_norm_w', 'w_a', 'q_norm_b', 'k_norm_b', 'sink_b', 'w_b', 'q_norm_c', 'k_norm_c', 'w_c', 'w_out',
               'norm2_w', 'peer_wq', 'peer_k1', 'peer_k2', 'peer_u', 'peer_v')


def kernel(x_prompt, x_sample, norm1_w, w_in, conv_w, conv_b, dt_bias_f, dt_bias_b, a_log_f, a_log_b, d_skip, ssm_norm_w, w_a, q_norm_b, k_norm_b, sink_b, w_b, q_norm_c, k_norm_c, w_c, w_out, norm2_w, peer_wq, peer_k1, peer_k2, peer_u, peer_v):
    stacked = dict(zip(_LAYER_KEYS, (norm1_w, w_in, conv_w, conv_b, dt_bias_f, dt_bias_b, a_log_f, a_log_b, d_skip,
                                     ssm_norm_w, w_a, q_norm_b, k_norm_b, sink_b, w_b, q_norm_c, k_norm_c, w_c, w_out,
                                     norm2_w, peer_wq, peer_k1, peer_k2, peer_u, peer_v)))
    layers = []
    for i in range(w_in.shape[0]):
        p = {k: v[i] for k, v in stacked.items()}
        p['u_packed'] = _pack_table(p['peer_u'])
        p['v_packed'] = _pack_table(p['peer_v'])
        layers.append(p)

    def trunk(x):
        for p in layers:
            x = _encoder_layer(x, p)
        return x

    return trunk(x_prompt), trunk(x_sample)
```

```python
import functools
import math

import jax
import jax.numpy as jnp
from jax import lax
from jax.experimental import pallas as pl
from jax.experimental.pallas import tpu as pltpu

D_MODEL = 1024
HEAD_DIM = 64
SSM_HEADS = 16
SSM_HEAD_DIM = 64
SSM_INNER = SSM_HEADS * SSM_HEAD_DIM
SSM_GROUPS = 2
SSM_STATE = 128
SSM_CHUNK = 128
CONV_WIDTH = 5
CONV_CH = SSM_INNER + 2 * SSM_GROUPS * SSM_STATE
SWA_HEADS = 16
SWA_KV_HEADS = 4
SWA_WINDOW = 128
SWA_BLOCK = 128
DIL_PATTERNS = ((128, 1), (512, 4), (2048, 16))
DIL_HEADS_PER_GROUP = 4
DIL_HEADS = DIL_HEADS_PER_GROUP * 3
DIL_OUT = DIL_HEADS_PER_GROUP * HEAD_DIM
PEER_HEADS = 8
PEER_N_KEYS = 128
PEER_EXPERTS = PEER_N_KEYS * PEER_N_KEYS
PEER_TOPK = 16
PEER_QDIM = 256
PEER_PICKS = PEER_HEADS * PEER_TOPK
N_BRANCHES = 3
ROPE_THETA = 10000.0
EPS = 1e-6
NEG_INF = -1e30
IN_SPLITS = (SSM_INNER, SSM_INNER, SSM_GROUPS * SSM_STATE, SSM_GROUPS * SSM_STATE, SSM_HEADS, SSM_HEADS,
             SWA_HEADS * HEAD_DIM, SWA_KV_HEADS * HEAD_DIM, SWA_KV_HEADS * HEAD_DIM,
             DIL_HEADS * HEAD_DIM, DIL_HEADS * HEAD_DIM, DIL_HEADS * HEAD_DIM,
             N_BRANCHES * D_MODEL)
IN_COLS = sum(IN_SPLITS)

LANES = 128
SUBLANES = 8
VMEM_LIMIT_BYTES = 56 * 1024 * 1024

PACK_WORDS = D_MODEL // 2
PACK_ROWS = PACK_WORDS // LANES
ROW_SUBLANES = D_MODEL // LANES
PEER_TB = 32


def _mm_kernel(x_ref, g_ref, w_ref, o_ref, xn_ref, *, norm):
    @pl.when(pl.program_id(1) == 0)
    def _():
        x = x_ref[...].astype(jnp.float32)
        if norm:
            x = x * lax.rsqrt(jnp.mean(x * x, axis=-1, keepdims=True) + EPS) * g_ref[...]
        xn_ref[...] = x.astype(jnp.bfloat16)

    o_ref[...] = jnp.dot(xn_ref[...], w_ref[...], preferred_element_type=jnp.float32)


def _matmul(x, w, gain=None, tm=1024, tn=512):
    m, k = x.shape
    n = w.shape[1]
    tm = min(tm, m)
    tn = min(tn, n)
    assert m % tm == 0
    n_pad = -n % tn
    wb = w.astype(jnp.bfloat16)
    if n_pad:
        wb = jnp.pad(wb, ((0, 0), (0, n_pad)))
    g = jnp.ones((1, k), jnp.float32) if gain is None else gain.reshape(1, k).astype(jnp.float32)
    out = pl.pallas_call(
        functools.partial(_mm_kernel, norm=gain is not None),
        grid=(m // tm, (n + n_pad) // tn),
        in_specs=[pl.BlockSpec((tm, k), lambda i, j: (i, 0)),
                  pl.BlockSpec((1, k), lambda i, j: (0, 0)),
                  pl.BlockSpec((k, tn), lambda i, j: (0, j))],
        out_specs=pl.BlockSpec((tm, tn), lambda i, j: (i, j)),
        out_shape=jax.ShapeDtypeStruct((m, n + n_pad), jnp.float32),
        scratch_shapes=[pltpu.VMEM((tm, k), jnp.bfloat16)],
        compiler_params=pltpu.CompilerParams(dimension_semantics=("parallel", "arbitrary"),
                                             vmem_limit_bytes=VMEM_LIMIT_BYTES),
        name="proj_matmul",
    )(x, g, wb)
    return out[:, :n] if n_pad else out


def _mixer_out_kernel(x_ref, ya_ref, yb_ref, yc_ref, gl_ref, wa_ref, wb_ref, wc_ref, wo_ref, o_ref):
    def proj(y_ref, w_ref):
        return jnp.dot(y_ref[...].astype(jnp.bfloat16), w_ref[...], preferred_element_type=jnp.float32)

    gates = jax.nn.sigmoid(gl_ref[...])
    merged = (gates[:, :D_MODEL] * proj(ya_ref, wa_ref) + gates[:, D_MODEL:2 * D_MODEL] * proj(yb_ref, wb_ref)
              + gates[:, 2 * D_MODEL:] * proj(yc_ref, wc_ref))
    o_ref[...] = x_ref[...] + jnp.dot(merged.astype(jnp.bfloat16), wo_ref[...], preferred_element_type=jnp.float32)


def _mixer_out(x, ya, yb, yc, gate_logits, w_a, w_b, w_c, w_out, tm=512):
    t = x.shape[0]
    assert t % tm == 0
    ws = [w.astype(jnp.bfloat16) for w in (w_a, w_b, w_c, w_out)]
    row = lambda a: pl.BlockSpec((tm, a.shape[1]), lambda i: (i, 0))
    full = lambda a: pl.BlockSpec(a.shape, lambda i: (0, 0))
    acts = (x, ya, yb, yc, gate_logits)
    return pl.pallas_call(
        _mixer_out_kernel,
        grid=(t // tm,),
        in_specs=[row(a) for a in acts] + [full(w) for w in ws],
        out_specs=row(x),
        out_shape=jax.ShapeDtypeStruct(x.shape, jnp.float32),
        compiler_params=pltpu.CompilerParams(dimension_semantics=("parallel",), vmem_limit_bytes=VMEM_LIMIT_BYTES),
        name="mixer_out",
    )(*acts, *ws)


def _pack_table(tab):
    e = tab.shape[0]
    tb = lax.bitcast_convert_type(tab.astype(jnp.bfloat16), jnp.uint16).astype(jnp.uint32)
    tb = tb.reshape(e, PACK_ROWS, 2, LANES)
    packed = tb[:, :, 0, :] | (tb[:, :, 1, :] << 16)
    return lax.bitcast_convert_type(packed, jnp.int32).reshape(e * PACK_ROWS, LANES)


def _peer_constants():
    col = jnp.arange(PEER_PICKS * ROW_SUBLANES)
    diag = (col[None, :] % ROW_SUBLANES == jnp.arange(ROW_SUBLANES)[:, None]).astype(jnp.float32)
    group = (col[:, None] // ROW_SUBLANES == jnp.arange(PEER_PICKS)[None, :]).astype(jnp.bfloat16)
    rep = (jnp.arange(PEER_TB * ROW_SUBLANES)[:, None] // ROW_SUBLANES == jnp.arange(PEER_TB)[None, :])
    return diag, group, group.T, rep.astype(jnp.bfloat16)


def _split3(x):
    hi = x.astype(jnp.bfloat16).astype(jnp.float32)
    r1 = x - hi
    mid = r1.astype(jnp.bfloat16).astype(jnp.float32)
    return hi, mid, r1 - mid


def _stack_pieces(pieces):
    return jnp.concatenate(list(pieces) + [jnp.zeros_like(pieces[0])], axis=1).astype(jnp.bfloat16)


def _sum_pieces(y):
    s = ROW_SUBLANES
    return y[0:s] + y[s:2 * s] + y[2 * s:3 * s]


def _gather_rows(idx_ref, tab_ref, t, raw_ref):
    for j in range(PEER_PICKS):
        e = idx_ref[t, j]
        src = pl.ds(pl.multiple_of(e * PACK_ROWS, PACK_ROWS), PACK_ROWS)
        raw_ref[pl.ds(j * PACK_ROWS, PACK_ROWS), :] = tab_ref[src, :]


def _token_rows(t):
    return pl.ds(pl.multiple_of(t * ROW_SUBLANES, ROW_SUBLANES), ROW_SUBLANES)


def _pipelined_tokens(gather, compute, bufs):
    gather(0, bufs[0])

    def step(t, carry):
        nxt = jnp.minimum(t + 1, PEER_TB - 1)
        for parity in range(2):
            @pl.when(t % 2 == parity)
            def _():
                compute(t, bufs[parity])
                gather(nxt, bufs[1 - parity])
        return carry

    lax.fori_loop(0, PEER_TB, step, 0)


def _peer_act_kernel(idx_ref, x_ref, gate_ref, diag_ref, group_ref, tab_ref, w_ref, raw0_ref, raw1_ref, x3_ref,
                     picked_ref):
    nt = (((1,), (1,)), ((), ()))
    x3_ref[...] = _stack_pieces(_split3(x_ref[...].reshape(PEER_TB, ROW_SUBLANES, LANES)))

    def compute(t, buf_ref):
        wmat = pltpu.bitcast(buf_ref[...], jnp.bfloat16)
        scores = _sum_pieces(lax.dot_general(x3_ref[t], wmat, nt, preferred_element_type=jnp.float32))
        picked_ref[_token_rows(t), :] = scores * diag_ref[...]

    _pipelined_tokens(functools.partial(_gather_rows, idx_ref, tab_ref), compute, (raw0_ref, raw1_ref))

    act8 = sum(jnp.dot(piece.astype(jnp.bfloat16), group_ref[...], preferred_element_type=jnp.float32)
               for piece in _split3(picked_ref[...]))
    act = jnp.sum(act8.reshape(PEER_TB, ROW_SUBLANES, PEER_PICKS), axis=1)
    gelu = 0.5 * act * (1.0 + lax.erf(act * (2.0 ** -0.5)))
    w_ref[...] = gate_ref[...] * gelu


def _peer_out_kernel(idx_ref, w_ref, res_ref, diag_ref, expand_ref, rep_ref, tab_ref, o_ref, raw0_ref, raw1_ref,
                     lhs_ref):
    pieces = []
    for piece in _split3(w_ref[...]):
        rep = jnp.dot(rep_ref[...], piece.astype(jnp.bfloat16), preferred_element_type=jnp.float32)
        wide = jnp.dot(rep.astype(jnp.bfloat16), expand_ref[...], preferred_element_type=jnp.float32)
        pieces.append(wide.reshape(PEER_TB, ROW_SUBLANES, PEER_PICKS * ROW_SUBLANES) * diag_ref[...][None])
    lhs_ref[...] = _stack_pieces(pieces)

    def compute(t, buf_ref):
        wmat = pltpu.bitcast(buf_ref[...], jnp.bfloat16)
        rows = _token_rows(t)
        o_ref[rows, :] = res_ref[rows, :] + _sum_pieces(jnp.dot(lhs_ref[t], wmat, preferred_element_type=jnp.float32))

    _pipelined_tokens(functools.partial(_gather_rows, idx_ref, tab_ref), compute, (raw0_ref, raw1_ref))


def _const_spec(shape):
    return pl.BlockSpec(shape, lambda i: (0,) * len(shape), pipeline_mode=pl.Buffered(1))


def _peer_experts(x_res, xn, idx, gate, u_packed, v_packed):
    t = xn.shape[0]
    assert t % PEER_TB == 0 and PEER_TB % 2 == 0
    grid = (t // PEER_TB,)
    params = pltpu.CompilerParams(dimension_semantics=("arbitrary",), vmem_limit_bytes=VMEM_LIMIT_BYTES)
    smem_spec = pl.BlockSpec((PEER_TB, PEER_PICKS), lambda i: (i, 0), memory_space=pltpu.SMEM)
    rows_spec = pl.BlockSpec((PEER_TB * ROW_SUBLANES, LANES), lambda i: (i, 0))
    picks_spec = pl.BlockSpec((PEER_TB, PEER_PICKS), lambda i: (i, 0))
    raw = pltpu.VMEM((PEER_PICKS * PACK_ROWS, LANES), jnp.int32)
    diag, group, expand, rep = _peer_constants()
    stacked = 4 * ROW_SUBLANES

    w = pl.pallas_call(
        _peer_act_kernel,
        grid=grid,
        in_specs=[smem_spec, rows_spec, picks_spec, _const_spec(diag.shape), _const_spec(group.shape),
                  _const_spec(u_packed.shape)],
        out_specs=picks_spec,
        out_shape=jax.ShapeDtypeStruct((t, PEER_PICKS), jnp.float32),
        scratch_shapes=[raw, raw, pltpu.VMEM((PEER_TB, stacked, LANES), jnp.bfloat16),
                        pltpu.VMEM((PEER_TB * ROW_SUBLANES, PEER_PICKS * ROW_SUBLANES), jnp.float32)],
        compiler_params=params,
        name="peer_act",
    )(idx, xn.reshape(t * ROW_SUBLANES, LANES), gate, diag, group, u_packed)

    out = pl.pallas_call(
        _peer_out_kernel,
        grid=grid,
        in_specs=[smem_spec, picks_spec, rows_spec, _const_spec(diag.shape), _const_spec(expand.shape),
                  _const_spec(rep.shape), _const_spec(v_packed.shape)],
        out_specs=rows_spec,
        out_shape=jax.ShapeDtypeStruct((t * ROW_SUBLANES, LANES), jnp.float32),
        scratch_shapes=[raw, raw, pltpu.VMEM((PEER_TB, stacked, PEER_PICKS * ROW_SUBLANES), jnp.bfloat16)],
        compiler_params=params,
        name="peer_out",
    )(idx, w, x_res.reshape(t * ROW_SUBLANES, LANES), diag, expand, rep, v_packed)
    return out.reshape(t, D_MODEL)


ATT_BLK = 128
HALF_HEAD = HEAD_DIM // 2


def _rope_tables(seq):
    inv_freq = ROPE_THETA ** (-jnp.arange(HALF_HEAD, dtype=jnp.float32) / HALF_HEAD)
    ang = jnp.arange(seq).astype(jnp.float32)[:, None] * inv_freq[None, :]
    cos, sin = jnp.cos(ang), jnp.sin(ang)
    return jnp.tile(jnp.concatenate([cos, cos], axis=1), (1, 2)), jnp.tile(jnp.concatenate([-sin, sin], axis=1), (1, 2))


def _norm_rope(a, gain, cos, sin):
    lane = lax.broadcasted_iota(jnp.int32, (a.shape[0], LANES), 1)
    low_head = lane < HEAD_DIM
    first_half = (lane % HEAD_DIM) < HALF_HEAD
    out = []
    for c in range(a.shape[1] // LANES):
        x = a[:, c * LANES:(c + 1) * LANES]
        sq = x * x
        lo = jnp.sum(jnp.where(low_head, sq, 0.0), axis=1, keepdims=True)
        hi = jnp.sum(jnp.where(low_head, 0.0, sq), axis=1, keepdims=True)
        y = x * lax.rsqrt(jnp.where(low_head, lo, hi) * (1.0 / HEAD_DIM) + EPS) * gain
        rot = jnp.where(first_half, pltpu.roll(y, LANES - HALF_HEAD, axis=1), pltpu.roll(y, HALF_HEAD, axis=1))
        out.append(y * cos + rot * sin)
    return out


def _attn_kernel(sink_ref, q_ref, kp_ref, kc_ref, kn_ref, vp_ref, vc_ref, vn_ref, cp_ref, cc_ref, cn_ref,
                 sp_ref, sc_ref, sn_ref, qg_ref, kg_ref, o_ref, *lse_refs, hq, hkv, window, use_sink):
    n = pl.program_id(2)
    nblk = pl.num_programs(2)
    nt = (((1,), (1,)), ((), ()))
    scale = HEAD_DIM ** -0.5
    q_chunks = _norm_rope(q_ref[...], qg_ref[...], cc_ref[...], sc_ref[...])
    k_parts = [_norm_rope(k_ref[...], kg_ref[...], c_ref[...], s_ref[...])
               for k_ref, c_ref, s_ref in ((kp_ref, cp_ref, sp_ref), (kc_ref, cc_ref, sc_ref), (kn_ref, cn_ref, sn_ref))]
    k_chunks = [jnp.concatenate([part[c] for part in k_parts], axis=0).astype(jnp.bfloat16)
                for c in range(hkv // 2)]
    v_all = jnp.concatenate([vp_ref[...], vc_ref[...], vn_ref[...]], axis=0).astype(jnp.bfloat16)

    row = lax.broadcasted_iota(jnp.int32, (ATT_BLK, 3 * ATT_BLK), 0)
    col = lax.broadcasted_iota(jnp.int32, (ATT_BLK, 3 * ATT_BLK), 1)
    rel = col - ATT_BLK - row
    first_col = jnp.where(n > 0, 0, ATT_BLK)
    end_col = jnp.where(n < nblk - 1, 3 * ATT_BLK, 2 * ATT_BLK)
    mask = (jnp.abs(rel) <= window) & (col >= first_col) & (col < end_col)
    lane = lax.broadcasted_iota(jnp.int32, (ATT_BLK, LANES), 1)
    halves = (lane < HEAD_DIM, lane >= HEAD_DIM)

    group = hq // hkv
    outs = [jnp.zeros((ATT_BLK, LANES), jnp.float32) for _ in range(hq // 2)]
    lses = [jnp.zeros((ATT_BLK, LANES), jnp.float32) for _ in range(hq // 2)]
    for h in range(hq):
        g = h // group
        qh = q_chunks[h // 2]
        if h % 2 != g % 2:
            qh = pltpu.roll(qh, HEAD_DIM, axis=1)
        qh = (jnp.where(halves[g % 2], qh, 0.0)).astype(jnp.bfloat16)
        s = lax.dot_general(qh, k_chunks[g // 2], nt, preferred_element_type=jnp.float32) * scale
        s = jnp.where(mask, s, NEG_INF)
        m = jnp.max(s, axis=1, keepdims=True)
        if use_sink:
            m = jnp.maximum(m, sink_ref[h])
        p = jnp.exp(s - m)
        den = jnp.sum(p, axis=1, keepdims=True)
        if use_sink:
            den = den + jnp.exp(sink_ref[h] - m)
        vg = v_all[:, (g // 2) * LANES:(g // 2 + 1) * LANES]
        o = jnp.dot(p.astype(jnp.bfloat16), vg, preferred_element_type=jnp.float32) / den
        if h % 2 != g % 2:
            o = pltpu.roll(o, HEAD_DIM, axis=1)
        outs[h // 2] = jnp.where(halves[h % 2], o, outs[h // 2])
        if lse_refs:
            lses[h // 2] = jnp.where(halves[h % 2], m + jnp.log(den), lses[h // 2])
    o_ref[...] = jnp.concatenate(outs, axis=1)
    if lse_refs:
        lse_refs[0][...] = jnp.concatenate(lses, axis=1)


def _banded_attention(src, cos, sin, q_gain, k_gain, sink, *, batch, seq, dil, hq, hkv, q_off, k_off, v_off, window,
                      want_lse):
    rows = seq // dil
    assert rows % ATT_BLK == 0 and src.shape[0] == batch * rows and src.shape[1] % dil == 0
    nblk = rows // ATT_BLK
    width = src.shape[1] // dil
    qw, kw = hq * HEAD_DIM, hkv * HEAD_DIM
    assert all((r * width + off) % w == 0 for r in range(dil) for off, w in ((q_off, qw), (k_off, kw), (v_off, kw)))

    def spec(w, off, shift):
        return pl.BlockSpec((ATT_BLK, w), lambda b, r, n: (b * nblk + jnp.clip(n + shift, 0, nblk - 1),
                                                           (r * width + off) // w))

    def table_spec(shift):
        return pl.BlockSpec((ATT_BLK, LANES), lambda b, r, n: (jnp.clip(n + shift, 0, nblk - 1), r))

    tile2 = lambda gain: jnp.tile(gain.astype(jnp.float32), 2).reshape(1, LANES)
    gain_spec = pl.BlockSpec((1, LANES), lambda b, r, n: (0, 0))
    out_spec = pl.BlockSpec((ATT_BLK, qw), lambda b, r, n: (b * nblk + n, r))
    out_shape = jax.ShapeDtypeStruct((batch * rows, dil * qw), jnp.float32)
    use_sink = sink is not None
    sink_arr = sink.astype(jnp.float32) if use_sink else jnp.zeros((hq,), jnp.float32)
    cos_d, sin_d = cos.reshape(rows, dil * LANES), sin.reshape(rows, dil * LANES)
    res = pl.pallas_call(
        functools.partial(_attn_kernel, hq=hq, hkv=hkv, window=window, use_sink=use_sink),
        grid=(batch, dil, nblk),
        in_specs=[pl.BlockSpec(memory_space=pltpu.SMEM), spec(qw, q_off, 0),
                  spec(kw, k_off, -1), spec(kw, k_off, 0), spec(kw, k_off, 1),
                  spec(kw, v_off, -1), spec(kw, v_off, 0), spec(kw, v_off, 1),
                  table_spec(-1), table_spec(0), table_spec(1), table_spec(-1), table_spec(0), table_spec(1),
                  gain_spec, gain_spec],
        out_specs=[out_spec, out_spec] if want_lse else [out_spec],
        out_shape=[out_shape, out_shape] if want_lse else [out_shape],
        compiler_params=pltpu.CompilerParams(dimension_semantics=("parallel", "parallel", "arbitrary"),
                                             vmem_limit_bytes=VMEM_LIMIT_BYTES),
        name="banded_attention",
    )(sink_arr, src, src, src, src, src, src, src, cos_d, cos_d, cos_d, sin_d, sin_d, sin_d, tile2(q_gain), tile2(k_gain))
    return res if want_lse else res[0]


def _window_attention(proj, cos, sin, q_gain, k_gain, sink, batch, seq):
    qw = SWA_HEADS * HEAD_DIM
    kw = SWA_KV_HEADS * HEAD_DIM
    return _banded_attention(proj, cos, sin, q_gain, k_gain, sink, batch=batch, seq=seq, dil=1, hq=SWA_HEADS,
                             hkv=SWA_KV_HEADS, q_off=0, k_off=qw, v_off=qw + kw, window=SWA_WINDOW, want_lse=False)


def _merge_kernel(*refs):
    n = len(DIL_PATTERNS)
    o_refs, l_refs, out_ref = refs[:n], refs[n:2 * n], refs[2 * n]
    lses = [r[...] for r in l_refs]
    m = functools.reduce(jnp.maximum, lses)
    es = [jnp.exp(l - m) for l in lses]
    den = functools.reduce(lambda a, b: a + b, es)
    out_ref[...] = functools.reduce(lambda a, b: a + b, [(e / den) * o[...] for e, o in zip(es, o_refs)])


def _dilated_attention(proj, cos, sin, q_gain, k_gain, batch, seq):
    t = batch * seq
    gw = DIL_HEADS_PER_GROUP * HEAD_DIM
    full = DIL_HEADS * HEAD_DIM
    outs, lses = [], []
    for gi, (win, dil) in enumerate(DIL_PATTERNS):
        o, lse = _banded_attention(proj.reshape(t // dil, dil * proj.shape[1]), cos, sin, q_gain, k_gain, None,
                                   batch=batch, seq=seq, dil=dil, hq=DIL_HEADS_PER_GROUP, hkv=DIL_HEADS_PER_GROUP,
                                   q_off=gi * gw, k_off=full + gi * gw, v_off=2 * full + gi * gw,
                                   window=win // (2 * dil), want_lse=True)
        outs.append(o.reshape(t, gw))
        lses.append(lse.reshape(t, gw))
    tm = 1024
    spec = pl.BlockSpec((tm, gw), lambda i: (i, 0))
    return pl.pallas_call(
        _merge_kernel, grid=(t // tm,), in_specs=[spec] * (2 * len(DIL_PATTERNS)), out_specs=spec,
        out_shape=jax.ShapeDtypeStruct((t, gw), jnp.float32),
        compiler_params=pltpu.CompilerParams(dimension_semantics=("parallel",)), name="dilated_merge",
    )(*outs, *lses)


def rms_norm(x, w):
    xf = x.astype(jnp.float32)
    y = xf * lax.rsqrt(jnp.mean(xf * xf, axis=-1, keepdims=True) + EPS)
    return (y * w.astype(jnp.float32)).astype(x.dtype)


def rope(x, pos):
    half = x.shape[-1] // 2
    inv_freq = ROPE_THETA ** (-jnp.arange(half, dtype=jnp.float32) / half)
    ang = pos.astype(jnp.float32)[:, None] * inv_freq[None, :]
    cos = jnp.cos(ang)[None, :, None, :]
    sin = jnp.sin(ang)[None, :, None, :]
    xf = x.astype(jnp.float32)
    x1, x2 = xf[..., :half], xf[..., half:]
    return jnp.concatenate([x1 * cos - x2 * sin, x2 * cos + x1 * sin], axis=-1).astype(x.dtype)


def split_cols(t, sizes):
    out, start = [], 0
    for s in sizes:
        out.append(t[..., start:start + s])
        start += s
    return out


def banded_attention(q, k, v, window, block, sink):
    b, l, hkv, g, d = q.shape
    pad = (-l) % block
    n = (l + pad) // block
    qb = jnp.pad(q, ((0, 0), (0, pad), (0, 0), (0, 0), (0, 0))).reshape(b, n, block, hkv, g, d)
    kv_pad = ((0, 0), (block, block + pad), (0, 0), (0, 0))
    kb = jnp.pad(k, kv_pad).reshape(b, n + 2, block, hkv, d)
    vb = jnp.pad(v, kv_pad).reshape(b, n + 2, block, hkv, d)
    kw = jnp.concatenate([kb[:, :-2], kb[:, 1:-1], kb[:, 2:]], axis=2)
    vw = jnp.concatenate([vb[:, :-2], vb[:, 1:-1], vb[:, 2:]], axis=2)
    blk = jnp.arange(n)[:, None, None]
    qpos = blk * block + jnp.arange(block)[None, :, None]
    kpos = (blk - 1) * block + jnp.arange(3 * block)[None, None, :]
    mask = (jnp.abs(qpos - kpos) <= window) & (kpos >= 0) & (kpos < l)
    s = jnp.einsum('bnqhgd,bnkhd->bnhgqk', qb, kw, preferred_element_type=jnp.float32) * (d ** -0.5)
    s = jnp.where(mask[None, :, None, None], s, NEG_INF)
    m = jnp.max(s, axis=-1, keepdims=True)
    if sink is not None:
        sk = sink.astype(jnp.float32).reshape(1, 1, hkv, g, 1, 1)
        m = jnp.maximum(m, sk)
    p = jnp.exp(s - m)
    den = jnp.sum(p, axis=-1, keepdims=True)
    if sink is not None:
        den = den + jnp.exp(sk - m)
    o = jnp.einsum('bnhgqk,bnkhd->bnhgqd', p.astype(v.dtype), vw, preferred_element_type=jnp.float32) / den
    o = o.transpose(0, 1, 4, 2, 3, 5).reshape(b, n * block, hkv, g, d)[:, :l]
    lse = (m + jnp.log(den))[..., 0].transpose(0, 1, 4, 2, 3).reshape(b, n * block, hkv, g)[:, :l]
    return o.astype(q.dtype), lse


def ssd_scan(x, dt, a_neg, bm, cm):
    b, l, h, p = x.shape
    g, n = bm.shape[2], bm.shape[3]
    j = h // g
    cl = SSM_CHUNK
    c = l // cl
    f32 = jnp.float32
    xdt = (x.astype(f32) * dt[..., None]).reshape(b, c, cl, g, j, p)
    da = (dt * a_neg).reshape(b, c, cl, g, j)
    da_cs = jnp.cumsum(da, axis=2)
    bmc = bm.astype(f32).reshape(b, c, cl, g, n)
    cmc = cm.astype(f32).reshape(b, c, cl, g, n)
    cs_t = jnp.moveaxis(da_cs, 2, -1)
    diff = cs_t[..., :, None] - cs_t[..., None, :]
    tri = jnp.tril(jnp.ones((cl, cl), dtype=bool))
    decay_mat = jnp.where(tri, jnp.exp(jnp.where(tri, diff, 0.0)), 0.0)
    cb = jnp.einsum('bcqgn,bcsgn->bcgqs', cmc, bmc)
    y_diag = jnp.einsum('bcgjqs,bcsgjp->bcqgjp', cb[:, :, :, None] * decay_mat, xdt)
    decay_to_end = jnp.exp(da_cs[:, :, -1:] - da_cs)
    states = jnp.einsum('bcsgn,bcsgjp->bcgjpn', bmc, xdt * decay_to_end[..., None])
    chunk_decay = jnp.exp(da_cs[:, :, -1])

    def step(h_state, inp):
        st, dec = inp
        return h_state * dec[..., None, None] + st, h_state

    h0 = jnp.zeros((b, g, j, p, n), f32)
    _, h_in = lax.scan(step, h0, (jnp.moveaxis(states, 1, 0), jnp.moveaxis(chunk_decay, 1, 0)))
    h_in = jnp.moveaxis(h_in, 0, 1)
    y_off = jnp.einsum('bcqgn,bcgjpn->bcqgjp', cmc, h_in) * jnp.exp(da_cs)[..., None]
    return (y_diag + y_off).reshape(b, l, h, p)


def ssd_mixer(z, xs, bm, cm, dt_f, dt_b, conv_w, conv_b, dt_bias_f, dt_bias_b, a_log_f, a_log_b, d_skip, norm_w):
    b, l = xs.shape[:2]
    xbc = jnp.concatenate([xs, bm, cm], axis=-1)
    xbc = lax.conv_general_dilated(xbc, conv_w[:, None, :].astype(xbc.dtype), window_strides=(1,),
                                   padding=[(CONV_WIDTH // 2, CONV_WIDTH // 2)],
                                   dimension_numbers=('NWC', 'WIO', 'NWC'), feature_group_count=CONV_CH)
    xbc = jax.nn.silu(xbc + conv_b)
    xs, bm, cm = split_cols(xbc, (SSM_INNER, SSM_GROUPS * SSM_STATE, SSM_GROUPS * SSM_STATE))
    xs = xs.reshape(b, l, SSM_HEADS, SSM_HEAD_DIM)
    bm = bm.reshape(b, l, SSM_GROUPS, SSM_STATE)
    cm = cm.reshape(b, l, SSM_GROUPS, SSM_STATE)
    dtf = jax.nn.softplus(dt_f.astype(jnp.float32) + dt_bias_f)
    dtb = jax.nn.softplus(dt_b.astype(jnp.float32) + dt_bias_b)
    y_fwd = ssd_scan(xs, dtf, -jnp.exp(a_log_f.astype(jnp.float32)), bm, cm)

    def flip(t):
        return jnp.flip(t, axis=1)

    y_bwd = flip(ssd_scan(flip(xs), flip(dtb), -jnp.exp(a_log_b.astype(jnp.float32)), flip(bm), flip(cm)))
    y = y_fwd + y_bwd + d_skip[:, None] * xs.astype(jnp.float32)
    y = y.reshape(b, l, SSM_INNER) * jax.nn.silu(z.astype(jnp.float32))
    return rms_norm(y, norm_w).astype(z.dtype)


def window_gqa(q, k, v, q_gain, k_gain, sink, pos):
    b, l = q.shape[:2]
    grp = SWA_HEADS // SWA_KV_HEADS
    q = rope(rms_norm(q.reshape(b, l, SWA_HEADS, HEAD_DIM), q_gain), pos)
    k = rope(rms_norm(k.reshape(b, l, SWA_KV_HEADS, HEAD_DIM), k_gain), pos)
    v = v.reshape(b, l, SWA_KV_HEADS, HEAD_DIM)
    o, _ = banded_attention(q.reshape(b, l, SWA_KV_HEADS, grp, HEAD_DIM), k, v, SWA_WINDOW, SWA_BLOCK,
                            sink.reshape(SWA_KV_HEADS, grp))
    return o.reshape(b, l, SWA_HEADS * HEAD_DIM)


def dilated_attention(q, k, v, q_gain, k_gain, pos):
    b, l = q.shape[:2]
    nh = DIL_HEADS_PER_GROUP
    q = rope(rms_norm(q.reshape(b, l, DIL_HEADS, HEAD_DIM), q_gain), pos)
    k = rope(rms_norm(k.reshape(b, l, DIL_HEADS, HEAD_DIM), k_gain), pos)
    v = v.reshape(b, l, DIL_HEADS, HEAD_DIM)
    outs, lses = [], []
    for gi, (win, dil) in enumerate(DIL_PATTERNS):
        sl = slice(gi * nh, (gi + 1) * nh)
        ls = l // dil

        def gather(t):
            return t[:, :, sl].reshape(b, ls, dil, nh, HEAD_DIM).transpose(0, 2, 1, 3, 4).reshape(b * dil, ls, nh, HEAD_DIM)

        half = win // (2 * dil)
        o, lse = banded_attention(gather(q)[:, :, :, None], gather(k), gather(v), half, half, None)
        outs.append(o[:, :, :, 0].reshape(b, dil, ls, nh, HEAD_DIM).transpose(0, 2, 1, 3, 4).reshape(b, l, nh, HEAD_DIM))
        lses.append(lse[..., 0].reshape(b, dil, ls, nh).transpose(0, 2, 1, 3).reshape(b, l, nh))
    wts = jax.nn.softmax(jnp.stack(lses, axis=0), axis=0)
    o = jnp.sum(wts[..., None] * jnp.stack(outs, axis=0).astype(jnp.float32), axis=0)
    return o.reshape(b, l, DIL_OUT).astype(q.dtype)


ROUTE_TM = 256
CAND_GROUPS = ((0, 16), (1, 8), (2, 8), (3, 8), (4, 8), (5, 8), (6, 8), (7, 8))
CAND_ROWS = sum(n for _, n in CAND_GROUPS) + SUBLANES
NO_PAIR = PEER_TOPK * PEER_TOPK
TOPK_SHIFT = PEER_TOPK.bit_length() - 1
assert 1 << TOPK_SHIFT == PEER_TOPK


def _cand_flat_ids():
    flat = []
    for a, n in CAND_GROUPS:
        flat += [a * PEER_TOPK + b if (a + 1) * (b + 1) <= PEER_TOPK else NO_PAIR for b in range(n)]
    flat += [a * PEER_TOPK for a in range(SUBLANES, PEER_TOPK)]
    return jnp.broadcast_to(jnp.array(flat, jnp.int32)[:, None], (CAND_ROWS, LANES))


def _top_rows(s, row_ids, k, sentinel):
    slot = lax.broadcasted_iota(jnp.int32, (k, LANES), 0)
    vals = jnp.zeros((k, LANES), jnp.float32)
    ids = jnp.zeros((k, LANES), jnp.int32)
    for r in range(k):
        m = jnp.max(s, axis=0, keepdims=True)
        i = jnp.min(jnp.where(s == m, row_ids, sentinel), axis=0, keepdims=True)
        vals = jnp.where(slot == r, m, vals)
        ids = jnp.where(slot == r, i, ids)
        s = jnp.where(row_ids == i, -jnp.inf, s)
    return vals, ids


def _lookup_rows(table, pos):
    slot = lax.broadcasted_iota(jnp.int32, table.shape, 0)
    return jnp.sum(jnp.where(slot == pos, table, 0), axis=0, keepdims=True)


def _peer_route_kernel(x_ref, g_ref, wq_ref, k1_ref, k2_ref, flat_ref, xn_ref, idx_ref, gate_ref, xb_ref):
    nt = (((1,), (1,)), ((), ()))

    @pl.when(pl.program_id(1) == 0)
    def _():
        x = x_ref[...]
        xn = x * lax.rsqrt(jnp.mean(x * x, axis=-1, keepdims=True) + EPS) * g_ref[...]
        xn_ref[...] = xn
        xb_ref[...] = xn.astype(jnp.bfloat16)

    q = jnp.dot(xb_ref[...], wq_ref[...], preferred_element_type=jnp.float32).astype(jnp.bfloat16)
    half = PEER_QDIM // 2
    s1 = lax.dot_general(k1_ref[...], q[:, :half], nt, preferred_element_type=jnp.float32)
    s2 = lax.dot_general(k2_ref[...], q[:, half:], nt, preferred_element_type=jnp.float32)
    key_ids = lax.broadcasted_iota(jnp.int32, (PEER_N_KEYS, LANES), 0)
    flat = flat_ref[...]
    for lt in range(ROUTE_TM // LANES):
        lanes = slice(lt * LANES, (lt + 1) * LANES)
        v1, i1 = _top_rows(s1[:, lanes], key_ids, PEER_TOPK, PEER_N_KEYS)
        v2, i2 = _top_rows(s2[:, lanes], key_ids, PEER_TOPK, PEER_N_KEYS)
        blocks = [v1[a:a + 1] + v2[:n] for a, n in CAND_GROUPS] + [v1[SUBLANES:] + v2[0:1]]
        cand = jnp.where(flat < NO_PAIR, jnp.concatenate(blocks, axis=0), -jnp.inf)
        top_s, top_flat = _top_rows(cand, flat, PEER_TOPK, 2 * NO_PAIR)
        ids = jnp.concatenate(
            [_lookup_rows(i1, top_flat[r:r + 1] >> TOPK_SHIFT) * PEER_N_KEYS + _lookup_rows(i2, top_flat[r:r + 1] & (PEER_TOPK - 1))
             for r in range(PEER_TOPK)], axis=0)
        e = jnp.exp(top_s - top_s[0:1])
        idx_ref[:, lanes] = ids
        gate_ref[:, lanes] = e / jnp.sum(e, axis=0, keepdims=True)


def _peer_route(x, gain, wq, k1, k2):
    t = x.shape[0]
    assert t % ROUTE_TM == 0
    xn, idx_t, gate_t = pl.pallas_call(
        _peer_route_kernel,
        grid=(t // ROUTE_TM, PEER_HEADS),
        in_specs=[pl.BlockSpec((ROUTE_TM, D_MODEL), lambda i, h: (i, 0)),
                  pl.BlockSpec((1, D_MODEL), lambda i, h: (0, 0)),
                  pl.BlockSpec((D_MODEL, PEER_QDIM), lambda i, h: (0, h)),
                  pl.BlockSpec((PEER_N_KEYS, PEER_QDIM // 2), lambda i, h: (0, 0)),
                  pl.BlockSpec((PEER_N_KEYS, PEER_QDIM // 2), lambda i, h: (0, 0)),
                  pl.BlockSpec((CAND_ROWS, LANES), lambda i, h: (0, 0))],
        out_specs=[pl.BlockSpec((ROUTE_TM, D_MODEL), lambda i, h: (i, 0)),
                   pl.BlockSpec((PEER_TOPK, ROUTE_TM), lambda i, h: (h, i)),
                   pl.BlockSpec((PEER_TOPK, ROUTE_TM), lambda i, h: (h, i))],
        out_shape=[jax.ShapeDtypeStruct((t, D_MODEL), jnp.float32),
                   jax.ShapeDtypeStruct((PEER_PICKS, t), jnp.int32),
                   jax.ShapeDtypeStruct((PEER_PICKS, t), jnp.float32)],
        scratch_shapes=[pltpu.VMEM((ROUTE_TM, D_MODEL), jnp.bfloat16)],
        compiler_params=pltpu.CompilerParams(dimension_semantics=("parallel", "arbitrary"),
                                             vmem_limit_bytes=VMEM_LIMIT_BYTES),
        name="peer_route",
    )(x, gain.reshape(1, D_MODEL), wq.astype(jnp.bfloat16), k1.astype(jnp.bfloat16), k2.astype(jnp.bfloat16),
      _cand_flat_ids())
    return xn, idx_t.T, gate_t.T


def _encoder_layer(x, p):
    b, l, _ = x.shape
    t = b * l
    xt = x.reshape(t, D_MODEL)
    w_in, gain = p['w_in'], p['norm1_w']
    ssm_w, swa_w, dil_w = sum(IN_SPLITS[:6]), sum(IN_SPLITS[6:9]), sum(IN_SPLITS[9:12])
    ssm_proj = _matmul(xt, w_in[:, :ssm_w], gain=gain, tn=384).reshape(b, l, ssm_w)
    swa_proj = _matmul(xt, w_in[:, ssm_w:ssm_w + swa_w], gain=gain, tn=768)
    dil_proj = _matmul(xt, w_in[:, ssm_w + swa_w:ssm_w + swa_w + dil_w], gain=gain, tn=768)
    gate_logits = _matmul(xt, w_in[:, ssm_w + swa_w + dil_w:], gain=gain, tn=1024)
    z, xs, bm, cm, dt_f, dt_b = split_cols(ssm_proj, IN_SPLITS[:6])
    y_a = ssd_mixer(z, xs, bm, cm, dt_f, dt_b, p['conv_w'], p['conv_b'], p['dt_bias_f'], p['dt_bias_b'],
                    p['a_log_f'], p['a_log_b'], p['d_skip'], p['ssm_norm_w']).reshape(t, SSM_INNER)
    cos, sin = _rope_tables(l)
    y_b = _window_attention(swa_proj, cos, sin, p['q_norm_b'], p['k_norm_b'], p['sink_b'], b, l)
    y_c = _dilated_attention(dil_proj, cos, sin, p['q_norm_c'], p['k_norm_c'], b, l)
    xt = _mixer_out(xt, y_a, y_b, y_c, gate_logits, p['w_a'], p['w_b'], p['w_c'], p['w_out'])
    xn, idx, gate = _peer_route(xt, p['norm2_w'], p['peer_wq'], p['peer_k1'], p['peer_k2'])
    xt = _peer_experts(xt, xn, idx, gate, p['u_packed'], p['v_packed'])
    return xt.reshape(b, l, D_MODEL)


_LAYER_KEYS = ('norm1_w', 'w_in', 'conv_w', 'conv_b', 'dt_bias_f', 'dt_bias_b', 'a_log_f', 'a_log_b', 'd_skip',
               'ssm_norm_w', 'w_a', 'q_norm_b', 'k_norm_b', 'sink_b', 'w_b', 'q_norm_c', 'k_norm_c', 'w_c', 'w_out',
               'norm2_w', 'peer_wq', 'peer_k1', 'peer_k2', 'peer_u', 'peer_v')


def kernel(x_prompt, x_sample, norm1_w, w_in, conv_w, conv_b, dt_bias_f, dt_bias_b, a_log_f, a_log_b, d_skip, ssm_norm_w, w_a, q_norm_b, k_norm_b, sink_b, w_b, q_norm_c, k_norm_c, w_c, w_out, norm2_w, peer_wq, peer_k1, peer_k2, peer_u, peer_v):
    stacked = dict(zip(_LAYER_KEYS, (norm1_w, w_in, conv_w, conv_b, dt_bias_f, dt_bias_b, a_log_f, a_log_b, d_skip,
                                     ssm_norm_w, w_a, q_norm_b, k_norm_b, sink_b, w_b, q_norm_c, k_norm_c, w_c, w_out,
                                     norm2_w, peer_wq, peer_k1, peer_k2, peer_u, peer_v)))
    layers = []
    for i in range(w_in.shape[0]):
        p = {k: v[i] for k, v in stacked.items()}
        p['u_packed'] = _pack_table(p['peer_u'])
        p['v_packed'] = _pack_table(p['peer_v'])
        layers.append(p)

    def trunk(x):
        for p in layers:
            x = _encoder_layer(x, p)
        return x

    return trunk(x_prompt), trunk(x_sample)
```

```python
import functools
import math

import jax
import jax.numpy as jnp
from jax import lax
from jax.experimental import pallas as pl
from jax.experimental.pallas import tpu as pltpu

D_MODEL = 1024
HEAD_DIM = 64
SSM_HEADS = 16
SSM_HEAD_DIM = 64
SSM_INNER = SSM_HEADS * SSM_HEAD_DIM
SSM_GROUPS = 2
SSM_STATE = 128
SSM_CHUNK = 128
CONV_WIDTH = 5
CONV_CH = SSM_INNER + 2 * SSM_GROUPS * SSM_STATE
SWA_HEADS = 16
SWA_KV_HEADS = 4
SWA_WINDOW = 128
SWA_BLOCK = 128
DIL_PATTERNS = ((128, 1), (512, 4), (2048, 16))
DIL_HEADS_PER_GROUP = 4
DIL_HEADS = DIL_HEADS_PER_GROUP * 3
DIL_OUT = DIL_HEADS_PER_GROUP * HEAD_DIM
PEER_HEADS = 8
PEER_N_KEYS = 128
PEER_EXPERTS = PEER_N_KEYS * PEER_N_KEYS
PEER_TOPK = 16
PEER_QDIM = 256
PEER_PICKS = PEER_HEADS * PEER_TOPK
N_BRANCHES = 3
ROPE_THETA = 10000.0
EPS = 1e-6
NEG_INF = -1e30
IN_SPLITS = (SSM_INNER, SSM_INNER, SSM_GROUPS * SSM_STATE, SSM_GROUPS * SSM_STATE, SSM_HEADS, SSM_HEADS,
             SWA_HEADS * HEAD_DIM, SWA_KV_HEADS * HEAD_DIM, SWA_KV_HEADS * HEAD_DIM,
             DIL_HEADS * HEAD_DIM, DIL_HEADS * HEAD_DIM, DIL_HEADS * HEAD_DIM,
             N_BRANCHES * D_MODEL)
IN_COLS = sum(IN_SPLITS)

LANES = 128
SUBLANES = 8
VMEM_LIMIT_BYTES = 56 * 1024 * 1024

PACK_WORDS = D_MODEL // 2
PACK_ROWS = PACK_WORDS // LANES
ROW_SUBLANES = D_MODEL // LANES
PEER_TB = 32


def _mm_kernel(x_ref, g_ref, w_ref, o_ref, xn_ref, *, norm):
    @pl.when(pl.program_id(1) == 0)
    def _():
        x = x_ref[...].astype(jnp.float32)
        if norm:
            x = x * lax.rsqrt(jnp.mean(x * x, axis=-1, keepdims=True) + EPS) * g_ref[...]
        xn_ref[...] = x.astype(jnp.bfloat16)

    o_ref[...] = jnp.dot(xn_ref[...], w_ref[...], preferred_element_type=jnp.float32)


def _matmul(x, w, gain=None, tm=1024, tn=512, keep_pad=False):
    m, k = x.shape
    n = w.shape[1]
    tm = min(tm, m)
    tn = min(tn, n)
    assert m % tm == 0
    n_pad = -n % tn
    wb = w.astype(jnp.bfloat16)
    if n_pad:
        wb = jnp.pad(wb, ((0, 0), (0, n_pad)))
    g = jnp.ones((1, k), jnp.float32) if gain is None else gain.reshape(1, k).astype(jnp.float32)
    out = pl.pallas_call(
        functools.partial(_mm_kernel, norm=gain is not None),
        grid=(m // tm, (n + n_pad) // tn),
        in_specs=[pl.BlockSpec((tm, k), lambda i, j: (i, 0)),
                  pl.BlockSpec((1, k), lambda i, j: (0, 0)),
                  pl.BlockSpec((k, tn), lambda i, j: (0, j))],
        out_specs=pl.BlockSpec((tm, tn), lambda i, j: (i, j)),
        out_shape=jax.ShapeDtypeStruct((m, n + n_pad), jnp.float32),
        scratch_shapes=[pltpu.VMEM((tm, k), jnp.bfloat16)],
        compiler_params=pltpu.CompilerParams(dimension_semantics=("parallel", "arbitrary"),
                                             vmem_limit_bytes=VMEM_LIMIT_BYTES),
        name="proj_matmul",
    )(x, g, wb)
    return out[:, :n] if n_pad and not keep_pad else out


def _mixer_out_kernel(x_ref, ya_ref, yb_ref, yc_ref, gl_ref, wa_ref, wb_ref, wc_ref, wo_ref, o_ref):
    def proj(y_ref, w_ref):
        return jnp.dot(y_ref[...].astype(jnp.bfloat16), w_ref[...], preferred_element_type=jnp.float32)

    gates = jax.nn.sigmoid(gl_ref[...])
    merged = (gates[:, :D_MODEL] * proj(ya_ref, wa_ref) + gates[:, D_MODEL:2 * D_MODEL] * proj(yb_ref, wb_ref)
              + gates[:, 2 * D_MODEL:] * proj(yc_ref, wc_ref))
    o_ref[...] = x_ref[...] + jnp.dot(merged.astype(jnp.bfloat16), wo_ref[...], preferred_element_type=jnp.float32)


def _mixer_out(x, ya, yb, yc, gate_logits, w_a, w_b, w_c, w_out, tm=512):
    t = x.shape[0]
    assert t % tm == 0
    ws = [w.astype(jnp.bfloat16) for w in (w_a, w_b, w_c, w_out)]
    row = lambda a: pl.BlockSpec((tm, a.shape[1]), lambda i: (i, 0))
    full = lambda a: pl.BlockSpec(a.shape, lambda i: (0, 0))
    acts = (x, ya, yb, yc, gate_logits)
    return pl.pallas_call(
        _mixer_out_kernel,
        grid=(t // tm,),
        in_specs=[row(a) for a in acts] + [full(w) for w in ws],
        out_specs=row(x),
        out_shape=jax.ShapeDtypeStruct(x.shape, jnp.float32),
        compiler_params=pltpu.CompilerParams(dimension_semantics=("parallel",), vmem_limit_bytes=VMEM_LIMIT_BYTES),
        name="mixer_out",
    )(*acts, *ws)


def _pack_table(tab):
    e = tab.shape[0]
    tb = lax.bitcast_convert_type(tab.astype(jnp.bfloat16), jnp.uint16).astype(jnp.uint32)
    tb = tb.reshape(e, PACK_ROWS, 2, LANES)
    packed = tb[:, :, 0, :] | (tb[:, :, 1, :] << 16)
    return lax.bitcast_convert_type(packed, jnp.int32).reshape(e * PACK_ROWS, LANES)


def _peer_constants():
    col = jnp.arange(PEER_PICKS * ROW_SUBLANES)
    diag = (col[None, :] % ROW_SUBLANES == jnp.arange(ROW_SUBLANES)[:, None]).astype(jnp.float32)
    group = (col[:, None] // ROW_SUBLANES == jnp.arange(PEER_PICKS)[None, :]).astype(jnp.bfloat16)
    return diag, group, group.T


def _split2(x):
    hi = x.astype(jnp.bfloat16).astype(jnp.float32)
    return hi, x - hi


def _stack_pieces(pieces):
    return jnp.concatenate(list(pieces), axis=1).astype(jnp.bfloat16)


def _sum_pieces(y):
    return y[:ROW_SUBLANES] + y[ROW_SUBLANES:]


def _gather_rows(idx_ref, tab_ref, t, raw_ref):
    for j in range(PEER_PICKS):
        src = pl.ds(pl.multiple_of(idx_ref[t, j], PACK_ROWS), PACK_ROWS)
        raw_ref[pl.ds(j * PACK_ROWS, PACK_ROWS), :] = tab_ref[src, :]


def _token_rows(t):
    return pl.ds(t * ROW_SUBLANES, ROW_SUBLANES)


def _pipelined_tokens(gather, compute, bufs):
    gather(0, bufs[0])
    for t in range(PEER_TB):
        compute(t, bufs[t % 2])
        if t + 1 < PEER_TB:
            gather(t + 1, bufs[(t + 1) % 2])


def _peer_act_kernel(idx_ref, x_ref, gate_ref, diag_ref, group_ref, tab_ref, w_ref, raw0_ref, raw1_ref, x3_ref,
                     picked_ref):
    nt = (((1,), (1,)), ((), ()))
    x3_ref[...] = _stack_pieces(_split2(x_ref[...].reshape(PEER_TB, ROW_SUBLANES, LANES)))

    def compute(t, buf_ref):
        wmat = pltpu.bitcast(buf_ref[...], jnp.bfloat16)
        scores = _sum_pieces(lax.dot_general(x3_ref[t], wmat, nt, preferred_element_type=jnp.float32))
        picked_ref[_token_rows(t), :] = scores * diag_ref[...]

    _pipelined_tokens(functools.partial(_gather_rows, idx_ref, tab_ref), compute, (raw0_ref, raw1_ref))

    act8 = sum(jnp.dot(piece.astype(jnp.bfloat16), group_ref[...], preferred_element_type=jnp.float32)
               for piece in _split2(picked_ref[...]))
    act = jnp.sum(act8.reshape(PEER_TB, ROW_SUBLANES, PEER_PICKS), axis=1)
    gelu = 0.5 * act * (1.0 + lax.erf(act * (2.0 ** -0.5)))
    w_ref[...] = gate_ref[...] * gelu


def _peer_out_kernel(idx_ref, w_ref, res_ref, diag_ref, expand_ref, tab_ref, o_ref, raw0_ref, raw1_ref, lhs_ref):
    wide = [jnp.dot(piece.astype(jnp.bfloat16), expand_ref[...], preferred_element_type=jnp.float32)
            for piece in _split2(w_ref[...])]
    diag = diag_ref[...]
    for t in range(PEER_TB):
        rows = [jnp.broadcast_to(wd[t:t + 1], diag.shape) * diag for wd in wide]
        lhs_ref[t] = jnp.concatenate(rows, axis=0).astype(jnp.bfloat16)

    def compute(t, buf_ref):
        wmat = pltpu.bitcast(buf_ref[...], jnp.bfloat16)
        rows = _token_rows(t)
        o_ref[rows, :] = res_ref[rows, :] + _sum_pieces(jnp.dot(lhs_ref[t], wmat, preferred_element_type=jnp.float32))

    _pipelined_tokens(functools.partial(_gather_rows, idx_ref, tab_ref), compute, (raw0_ref, raw1_ref))


def _const_spec(shape):
    return pl.BlockSpec(shape, lambda i: (0,) * len(shape), pipeline_mode=pl.Buffered(1))


def _peer_experts(x_res, xn, idx, gate, u_packed, v_packed):
    t = xn.shape[0]
    assert t % PEER_TB == 0 and PEER_TB % 2 == 0
    grid = (t // PEER_TB,)
    params = pltpu.CompilerParams(dimension_semantics=("arbitrary",), vmem_limit_bytes=VMEM_LIMIT_BYTES)
    smem_spec = pl.BlockSpec((PEER_TB, PEER_PICKS), lambda i: (i, 0), memory_space=pltpu.SMEM)
    rows_spec = pl.BlockSpec((PEER_TB * ROW_SUBLANES, LANES), lambda i: (i, 0))
    picks_spec = pl.BlockSpec((PEER_TB, PEER_PICKS), lambda i: (i, 0))
    raw = pltpu.VMEM((PEER_PICKS * PACK_ROWS, LANES), jnp.int32)
    diag, group, expand = _peer_constants()
    stacked = 2 * ROW_SUBLANES

    w = pl.pallas_call(
        _peer_act_kernel,
        grid=grid,
        in_specs=[smem_spec, rows_spec, picks_spec, _const_spec(diag.shape), _const_spec(group.shape),
                  _const_spec(u_packed.shape)],
        out_specs=picks_spec,
        out_shape=jax.ShapeDtypeStruct((t, PEER_PICKS), jnp.float32),
        scratch_shapes=[raw, raw, pltpu.VMEM((PEER_TB, stacked, LANES), jnp.bfloat16),
                        pltpu.VMEM((PEER_TB * ROW_SUBLANES, PEER_PICKS * ROW_SUBLANES), jnp.float32)],
        compiler_params=params,
        name="peer_act",
    )(idx, xn.reshape(t * ROW_SUBLANES, LANES), gate, diag, group, u_packed)

    out = pl.pallas_call(
        _peer_out_kernel,
        grid=grid,
        in_specs=[smem_spec, picks_spec, rows_spec, _const_spec(diag.shape), _const_spec(expand.shape),
                  _const_spec(v_packed.shape)],
        out_specs=rows_spec,
        out_shape=jax.ShapeDtypeStruct((t * ROW_SUBLANES, LANES), jnp.float32),
        scratch_shapes=[raw, raw, pltpu.VMEM((PEER_TB, stacked, PEER_PICKS * ROW_SUBLANES), jnp.bfloat16)],
        compiler_params=params,
        name="peer_out",
    )(idx, w, x_res.reshape(t * ROW_SUBLANES, LANES), diag, expand, v_packed)
    return out.reshape(t, D_MODEL)


ATT_BLK = 128
HALF_HEAD = HEAD_DIM // 2


def _rope_tables(seq):
    inv_freq = ROPE_THETA ** (-jnp.arange(HALF_HEAD, dtype=jnp.float32) / HALF_HEAD)
    ang = jnp.arange(seq).astype(jnp.float32)[:, None] * inv_freq[None, :]
    cos, sin = jnp.cos(ang), jnp.sin(ang)
    return jnp.tile(jnp.concatenate([cos, cos], axis=1), (1, 2)), jnp.tile(jnp.concatenate([-sin, sin], axis=1), (1, 2))


def _norm_rope(a, gain, cos, sin):
    lane = lax.broadcasted_iota(jnp.int32, (a.shape[0], LANES), 1)
    low_head = lane < HEAD_DIM
    first_half = (lane % HEAD_DIM) < HALF_HEAD
    out = []
    for c in range(a.shape[1] // LANES):
        x = a[:, c * LANES:(c + 1) * LANES]
        sq = x * x
        lo = jnp.sum(jnp.where(low_head, sq, 0.0), axis=1, keepdims=True)
        hi = jnp.sum(jnp.where(low_head, 0.0, sq), axis=1, keepdims=True)
        y = x * lax.rsqrt(jnp.where(low_head, lo, hi) * (1.0 / HEAD_DIM) + EPS) * gain
        rot = jnp.where(first_half, pltpu.roll(y, LANES - HALF_HEAD, axis=1), pltpu.roll(y, HALF_HEAD, axis=1))
        out.append(y * cos + rot * sin)
    return out


def _attn_kernel(sink_ref, q_ref, kp_ref, kc_ref, kn_ref, vp_ref, vc_ref, vn_ref, cp_ref, cc_ref, cn_ref,
                 sp_ref, sc_ref, sn_ref, qg_ref, kg_ref, o_ref, *lse_refs, hq, hkv, window, use_sink):
    n = pl.program_id(2)
    nblk = pl.num_programs(2)
    nt = (((1,), (1,)), ((), ()))
    scale = HEAD_DIM ** -0.5
    q_chunks = _norm_rope(q_ref[...], qg_ref[...], cc_ref[...], sc_ref[...])
    k_parts = [_norm_rope(k_ref[...], kg_ref[...], c_ref[...], s_ref[...])
               for k_ref, c_ref, s_ref in ((kp_ref, cp_ref, sp_ref), (kc_ref, cc_ref, sc_ref), (kn_ref, cn_ref, sn_ref))]
    k_chunks = [jnp.concatenate([part[c] for part in k_parts], axis=0).astype(jnp.bfloat16)
                for c in range(hkv // 2)]
    v_all = jnp.concatenate([vp_ref[...], vc_ref[...], vn_ref[...]], axis=0).astype(jnp.bfloat16)

    row = lax.broadcasted_iota(jnp.int32, (ATT_BLK, 3 * ATT_BLK), 0)
    col = lax.broadcasted_iota(jnp.int32, (ATT_BLK, 3 * ATT_BLK), 1)
    rel = col - ATT_BLK - row
    first_col = jnp.where(n > 0, 0, ATT_BLK)
    end_col = jnp.where(n < nblk - 1, 3 * ATT_BLK, 2 * ATT_BLK)
    mask = (jnp.abs(rel) <= window) & (col >= first_col) & (col < end_col)
    lane = lax.broadcasted_iota(jnp.int32, (ATT_BLK, LANES), 1)
    halves = (lane < HEAD_DIM, lane >= HEAD_DIM)

    group = hq // hkv
    outs = [jnp.zeros((ATT_BLK, LANES), jnp.float32) for _ in range(hq // 2)]
    lses = [jnp.zeros((ATT_BLK, LANES), jnp.float32) for _ in range(hq // 2)]
    for h in range(hq):
        g = h // group
        qh = q_chunks[h // 2]
        if h % 2 != g % 2:
            qh = pltpu.roll(qh, HEAD_DIM, axis=1)
        qh = (jnp.where(halves[g % 2], qh, 0.0)).astype(jnp.bfloat16)
        s = lax.dot_general(qh, k_chunks[g // 2], nt, preferred_element_type=jnp.float32) * scale
        s = jnp.where(mask, s, NEG_INF)
        m = jnp.max(s, axis=1, keepdims=True)
        if use_sink:
            m = jnp.maximum(m, sink_ref[h])
        p = jnp.exp(s - m)
        den = jnp.sum(p, axis=1, keepdims=True)
        if use_sink:
            den = den + jnp.exp(sink_ref[h] - m)
        vg = v_all[:, (g // 2) * LANES:(g // 2 + 1) * LANES]
        o = jnp.dot(p.astype(jnp.bfloat16), vg, preferred_element_type=jnp.float32) / den
        if h % 2 != g % 2:
            o = pltpu.roll(o, HEAD_DIM, axis=1)
        outs[h // 2] = jnp.where(halves[h % 2], o, outs[h // 2])
        if lse_refs:
            lses[h // 2] = jnp.where(halves[h % 2], m + jnp.log(den), lses[h // 2])
    o_ref[...] = jnp.concatenate(outs, axis=1)
    if lse_refs:
        lse_refs[0][...] = jnp.concatenate(lses, axis=1)


def _banded_attention(src, cos, sin, q_gain, k_gain, sink, *, batch, seq, dil, hq, hkv, q_off, k_off, v_off, window,
                      want_lse):
    rows = seq // dil
    assert rows % ATT_BLK == 0 and src.shape[0] == batch * rows and src.shape[1] % dil == 0
    nblk = rows // ATT_BLK
    width = src.shape[1] // dil
    qw, kw = hq * HEAD_DIM, hkv * HEAD_DIM
    assert all((r * width + off) % w == 0 for r in range(dil) for off, w in ((q_off, qw), (k_off, kw), (v_off, kw)))

    def spec(w, off, shift):
        return pl.BlockSpec((ATT_BLK, w), lambda b, r, n: (b * nblk + jnp.clip(n + shift, 0, nblk - 1),
                                                           (r * width + off) // w))

    def table_spec(shift):
        return pl.BlockSpec((ATT_BLK, LANES), lambda b, r, n: (jnp.clip(n + shift, 0, nblk - 1), r))

    tile2 = lambda gain: jnp.tile(gain.astype(jnp.float32), 2).reshape(1, LANES)
    gain_spec = pl.BlockSpec((1, LANES), lambda b, r, n: (0, 0))
    out_spec = pl.BlockSpec((ATT_BLK, qw), lambda b, r, n: (b * nblk + n, r))
    out_shape = jax.ShapeDtypeStruct((batch * rows, dil * qw), jnp.float32)
    use_sink = sink is not None
    sink_arr = sink.astype(jnp.float32) if use_sink else jnp.zeros((hq,), jnp.float32)
    cos_d, sin_d = cos.reshape(rows, dil * LANES), sin.reshape(rows, dil * LANES)
    res = pl.pallas_call(
        functools.partial(_attn_kernel, hq=hq, hkv=hkv, window=window, use_sink=use_sink),
        grid=(batch, dil, nblk),
        in_specs=[pl.BlockSpec(memory_space=pltpu.SMEM), spec(qw, q_off, 0),
                  spec(kw, k_off, -1), spec(kw, k_off, 0), spec(kw, k_off, 1),
                  spec(kw, v_off, -1), spec(kw, v_off, 0), spec(kw, v_off, 1),
                  table_spec(-1), table_spec(0), table_spec(1), table_spec(-1), table_spec(0), table_spec(1),
                  gain_spec, gain_spec],
        out_specs=[out_spec, out_spec] if want_lse else [out_spec],
        out_shape=[out_shape, out_shape] if want_lse else [out_shape],
        compiler_params=pltpu.CompilerParams(dimension_semantics=("parallel", "parallel", "arbitrary"),
                                             vmem_limit_bytes=VMEM_LIMIT_BYTES),
        name="banded_attention",
    )(sink_arr, src, src, src, src, src, src, src, cos_d, cos_d, cos_d, sin_d, sin_d, sin_d, tile2(q_gain), tile2(k_gain))
    return res if want_lse else res[0]


def _window_attention(proj, cos, sin, q_gain, k_gain, sink, batch, seq):
    qw = SWA_HEADS * HEAD_DIM
    kw = SWA_KV_HEADS * HEAD_DIM
    return _banded_attention(proj, cos, sin, q_gain, k_gain, sink, batch=batch, seq=seq, dil=1, hq=SWA_HEADS,
                             hkv=SWA_KV_HEADS, q_off=0, k_off=qw, v_off=qw + kw, window=SWA_WINDOW, want_lse=False)


def _merge_kernel(*refs):
    n = len(DIL_PATTERNS)
    o_refs, l_refs, out_ref = refs[:n], refs[n:2 * n], refs[2 * n]
    lses = [r[...] for r in l_refs]
    m = functools.reduce(jnp.maximum, lses)
    es = [jnp.exp(l - m) for l in lses]
    den = functools.reduce(lambda a, b: a + b, es)
    out_ref[...] = functools.reduce(lambda a, b: a + b, [(e / den) * o[...] for e, o in zip(es, o_refs)])


def _dilated_attention(proj, cos, sin, q_gain, k_gain, batch, seq):
    t = batch * seq
    gw = DIL_HEADS_PER_GROUP * HEAD_DIM
    full = DIL_HEADS * HEAD_DIM
    outs, lses = [], []
    for gi, (win, dil) in enumerate(DIL_PATTERNS):
        o, lse = _banded_attention(proj.reshape(t // dil, dil * proj.shape[1]), cos, sin, q_gain, k_gain, None,
                                   batch=batch, seq=seq, dil=dil, hq=DIL_HEADS_PER_GROUP, hkv=DIL_HEADS_PER_GROUP,
                                   q_off=gi * gw, k_off=full + gi * gw, v_off=2 * full + gi * gw,
                                   window=win // (2 * dil), want_lse=True)
        outs.append(o.reshape(t, gw))
        lses.append(lse.reshape(t, gw))
    tm = 1024
    spec = pl.BlockSpec((tm, gw), lambda i: (i, 0))
    return pl.pallas_call(
        _merge_kernel, grid=(t // tm,), in_specs=[spec] * (2 * len(DIL_PATTERNS)), out_specs=spec,
        out_shape=jax.ShapeDtypeStruct((t, gw), jnp.float32),
        compiler_params=pltpu.CompilerParams(dimension_semantics=("parallel",)), name="dilated_merge",
    )(*outs, *lses)


HALO = SUBLANES
DT_LANES = LANES


def _conv_silu(prev_ref, cur_ref, next_ref, w_ref, b_ref, has_prev, has_next):
    prev = jnp.where(has_prev, prev_ref[...], 0.0)
    nxt = jnp.where(has_next, next_ref[...], 0.0)
    cat = jnp.concatenate([prev, cur_ref[...], nxt], axis=0)
    w = w_ref[...]
    acc = b_ref[...]
    for k in range(CONV_WIDTH):
        start = HALO + k - CONV_WIDTH // 2
        acc = acc + cat[start:start + SSM_CHUNK, :] * w[k:k + 1, :]
    return acc * jax.nn.sigmoid(acc)


def _lane_col(a, j):
    return jnp.broadcast_to(a[:, j:j + 1], a.shape)


def _ssd_kernel(xp_ref, xc_ref, xn_ref, bp_ref, bc_ref, bn_ref, dt_ref, z_ref, part_ref, wx_ref, bx_ref, wb_ref,
                bb_ref, dtb_ref, aneg_ref, skip_ref, nw_ref, o_ref, h_ref, *, rev):
    c = pl.program_id(1)
    nchunk = pl.num_programs(1)
    first, last = c == 0, c == nchunk - 1
    has_prev = jnp.logical_not(last) if rev else jnp.logical_not(first)
    has_next = jnp.logical_not(first) if rev else jnp.logical_not(last)

    @pl.when(first)
    def _():
        h_ref[...] = jnp.zeros_like(h_ref)

    xs = _conv_silu(xp_ref, xc_ref, xn_ref, wx_ref, bx_ref, has_prev, has_next)
    bc = _conv_silu(bp_ref, bc_ref, bn_ref, wb_ref, bb_ref, has_prev, has_next)
    gn = SSM_GROUPS * SSM_STATE
    bm, cm = bc[:, :gn], bc[:, gn:]

    dt_raw = dt_ref[...]
    if rev:
        dt_raw = pltpu.roll(dt_raw, DT_LANES - SSM_HEADS, axis=1)
    xb = dt_raw + dtb_ref[...]
    dt = jnp.maximum(xb, 0.0) + jnp.log(1.0 + jnp.exp(-jnp.abs(xb)))
    da = dt * aneg_ref[...]

    row = lax.broadcasted_iota(jnp.int32, (SSM_CHUNK, SSM_CHUNK), 0)
    col = lax.broadcasted_iota(jnp.int32, (SSM_CHUNK, SSM_CHUNK), 1)
    keep = (col >= row) if rev else (col <= row)
    cs = jnp.dot(keep.astype(jnp.float32), da, precision=lax.Precision.HIGHEST,
                 preferred_element_type=jnp.float32)
    cs_t = cs.T
    end_row = 0 if rev else SSM_CHUNK - 1

    lane = lax.broadcasted_iota(jnp.int32, (SSM_CHUNK, LANES), 1)
    low = lane < SSM_HEAD_DIM
    nt = (((1,), (1,)), ((), ()))
    bf = jnp.bfloat16
    heads_per_group = SSM_HEADS // SSM_GROUPS
    y_chunks = []
    for pair in range(SSM_HEADS // 2):
        g = (2 * pair) // heads_per_group
        bm_g = bm[:, g * SSM_STATE:(g + 1) * SSM_STATE]
        cm_g = cm[:, g * SSM_STATE:(g + 1) * SSM_STATE]
        if pair % (heads_per_group // 2) == 0:
            cb = lax.dot_general(cm_g.astype(bf), bm_g.astype(bf), nt, preferred_element_type=jnp.float32)
            bm_t = bm_g.T.astype(bf)
        xs_c = xs[:, pair * LANES:(pair + 1) * LANES]
        per_head = []
        for j in (2 * pair, 2 * pair + 1):
            ccol = _lane_col(cs, j)
            dec = jnp.where(keep, jnp.exp(jnp.where(keep, ccol - cs_t[j:j + 1, :], 0.0)), 0.0)
            total = ccol[end_row:end_row + 1, :]
            per_head.append(((cb * dec).astype(bf), _lane_col(dt, j), jnp.exp(ccol), jnp.exp(total - ccol),
                             jnp.exp(total)))
        (m_a, d_a, e_a, t_a, g_a), (m_b, d_b, e_b, t_b, g_b) = per_head
        xdt = xs_c * jnp.where(low, d_a, d_b)
        xdt_b = xdt.astype(bf)
        y_diag = jnp.where(low, jnp.dot(m_a, xdt_b, preferred_element_type=jnp.float32),
                           jnp.dot(m_b, xdt_b, preferred_element_type=jnp.float32))
        h = h_ref[pair]
        y_off = jnp.dot(cm_g.astype(bf), h.astype(bf), preferred_element_type=jnp.float32) * jnp.where(low, e_a, e_b)
        states = jnp.dot(bm_t, (xdt * jnp.where(low, t_a, t_b)).astype(bf), preferred_element_type=jnp.float32)
        h_ref[pair] = h * jnp.where(low, g_a, g_b) + states
        y = y_diag + y_off
        if not rev:
            y = y + xs_c * skip_ref[:, pair * LANES:(pair + 1) * LANES]
        y_chunks.append(y)
    y = jnp.concatenate(y_chunks, axis=1)
    if not rev:
        o_ref[...] = y
    else:
        z = z_ref[...]
        y = (y + part_ref[...]) * (z * jax.nn.sigmoid(z))
        o_ref[...] = y * lax.rsqrt(jnp.mean(y * y, axis=-1, keepdims=True) + EPS) * nw_ref[...]


def _ssd_mixer(proj, p, batch, seq):
    nchunk = seq // SSM_CHUNK
    assert seq % SSM_CHUNK == 0
    f32 = jnp.float32
    halo_per_chunk = SSM_CHUNK // HALO
    xcol = 1
    bc_w = 2 * SSM_GROUPS * SSM_STATE
    bc_col = (2 * SSM_INNER) // bc_w
    dt_col = (2 * SSM_INNER + bc_w) // DT_LANES
    pad16 = lambda a: jnp.pad(a.astype(f32), (0, LANES - SSM_HEADS)).reshape(1, LANES)
    per_lane = lambda a: jnp.repeat(a.astype(f32), SSM_HEAD_DIM).reshape(1, SSM_INNER)
    wx, wb = p['conv_w'][:, :SSM_INNER].astype(f32), p['conv_w'][:, SSM_INNER:].astype(f32)
    bx, bb = p['conv_b'][:SSM_INNER].reshape(1, -1).astype(f32), p['conv_b'][SSM_INNER:].reshape(1, -1).astype(f32)
    skip = per_lane(p['d_skip'])
    nw = p['ssm_norm_w'].reshape(1, SSM_INNER).astype(f32)

    def run(rev, part):
        pos = (lambda c: nchunk - 1 - c) if rev else (lambda c: c)

        def chunk(w, colblk):
            return pl.BlockSpec((SSM_CHUNK, w), lambda b, c: (b * nchunk + pos(c), colblk))

        def halo(w, colblk, side):
            def index(b, c):
                r = (b * nchunk + pos(c)) * halo_per_chunk
                r = r - 1 if side < 0 else r + halo_per_chunk
                return (jnp.clip(r, 0, batch * nchunk * halo_per_chunk - 1), colblk)
            return pl.BlockSpec((HALO, w), index)

        const = lambda a: pl.BlockSpec(a.shape, lambda b, c: (0, 0))
        dtb = pad16(p['dt_bias_b'] if rev else p['dt_bias_f'])
        aneg = pad16(-jnp.exp((p['a_log_b'] if rev else p['a_log_f']).astype(f32)))
        consts = (wx, bx, wb, bb, dtb, aneg, skip, nw)
        return pl.pallas_call(
            functools.partial(_ssd_kernel, rev=rev),
            grid=(batch, nchunk),
            in_specs=[halo(SSM_INNER, xcol, -1), chunk(SSM_INNER, xcol), halo(SSM_INNER, xcol, 1),
                      halo(bc_w, bc_col, -1), chunk(bc_w, bc_col), halo(bc_w, bc_col, 1),
                      chunk(DT_LANES, dt_col), chunk(SSM_INNER, 0), chunk(SSM_INNER, 0)] + [const(a) for a in consts],
            out_specs=chunk(SSM_INNER, 0),
            out_shape=jax.ShapeDtypeStruct((batch * seq, SSM_INNER), f32),
            scratch_shapes=[pltpu.VMEM((SSM_HEADS // 2, SSM_STATE, LANES), f32)],
            compiler_params=pltpu.CompilerParams(dimension_semantics=("parallel", "arbitrary"),
                                                 vmem_limit_bytes=VMEM_LIMIT_BYTES),
            name="ssd_bwd" if rev else "ssd_fwd",
        )(proj, proj, proj, proj, proj, proj, proj, proj, part, *consts)

    return run(True, run(False, proj))


ROUTE_TM = 256
CAND_GROUPS = ((0, 16), (1, 8), (2, 8), (3, 8), (4, 8), (5, 8), (6, 8), (7, 8))
CAND_ROWS = sum(n for _, n in CAND_GROUPS) + SUBLANES
NO_PAIR = PEER_TOPK * PEER_TOPK
TOPK_SHIFT = PEER_TOPK.bit_length() - 1
assert 1 << TOPK_SHIFT == PEER_TOPK


def _cand_flat_ids():
    flat = []
    for a, n in CAND_GROUPS:
        flat += [a * PEER_TOPK + b if (a + 1) * (b + 1) <= PEER_TOPK else NO_PAIR for b in range(n)]
    flat += [a * PEER_TOPK for a in range(SUBLANES, PEER_TOPK)]
    return jnp.broadcast_to(jnp.array(flat, jnp.int32)[:, None], (CAND_ROWS, LANES))


def _top_rows(s, row_ids, k, sentinel):
    slot = lax.broadcasted_iota(jnp.int32, (k, LANES), 0)
    vals = jnp.zeros((k, LANES), jnp.float32)
    ids = jnp.zeros((k, LANES), jnp.int32)
    for r in range(k):
        m = jnp.max(s, axis=0, keepdims=True)
        i = jnp.min(jnp.where(s == m, row_ids, sentinel), axis=0, keepdims=True)
        vals = jnp.where(slot == r, m, vals)
        ids = jnp.where(slot == r, i, ids)
        s = jnp.where(row_ids == i, -jnp.inf, s)
    return vals, ids


def _lookup_rows(table, pos):
    slot = lax.broadcasted_iota(jnp.int32, table.shape, 0)
    return jnp.sum(jnp.where(slot == pos, table, 0), axis=0, keepdims=True)


def _peer_route_kernel(x_ref, g_ref, wq_ref, k1_ref, k2_ref, flat_ref, xn_ref, idx_ref, gate_ref, xb_ref):
    nt = (((1,), (1,)), ((), ()))

    @pl.when(pl.program_id(1) == 0)
    def _():
        x = x_ref[...]
        xn = x * lax.rsqrt(jnp.mean(x * x, axis=-1, keepdims=True) + EPS) * g_ref[...]
        xn_ref[...] = xn
        xb_ref[...] = xn.astype(jnp.bfloat16)

    q = jnp.dot(xb_ref[...], wq_ref[...], preferred_element_type=jnp.float32).astype(jnp.bfloat16)
    half = PEER_QDIM // 2
    s1 = lax.dot_general(k1_ref[...], q[:, :half], nt, preferred_element_type=jnp.float32)
    s2 = lax.dot_general(k2_ref[...], q[:, half:], nt, preferred_element_type=jnp.float32)
    key_ids = lax.broadcasted_iota(jnp.int32, (PEER_N_KEYS, LANES), 0)
    flat = flat_ref[...]
    for lt in range(ROUTE_TM // LANES):
        lanes = slice(lt * LANES, (lt + 1) * LANES)
        v1, i1 = _top_rows(s1[:, lanes], key_ids, PEER_TOPK, PEER_N_KEYS)
        v2, i2 = _top_rows(s2[:, lanes], key_ids, PEER_TOPK, PEER_N_KEYS)
        blocks = [v1[a:a + 1] + v2[:n] for a, n in CAND_GROUPS] + [v1[SUBLANES:] + v2[0:1]]
        cand = jnp.where(flat < NO_PAIR, jnp.concatenate(blocks, axis=0), -jnp.inf)
        top_s, top_flat = _top_rows(cand, flat, PEER_TOPK, 2 * NO_PAIR)
        ids = jnp.concatenate(
            [_lookup_rows(i1, top_flat[r:r + 1] >> TOPK_SHIFT) * PEER_N_KEYS + _lookup_rows(i2, top_flat[r:r + 1] & (PEER_TOPK - 1))
             for r in range(PEER_TOPK)], axis=0)
        e = jnp.exp(top_s - top_s[0:1])
        idx_ref[:, lanes] = ids * PACK_ROWS
        gate_ref[:, lanes] = e / jnp.sum(e, axis=0, keepdims=True)


def _peer_route(x, gain, wq, k1, k2):
    t = x.shape[0]
    assert t % ROUTE_TM == 0
    xn, idx_t, gate_t = pl.pallas_call(
        _peer_route_kernel,
        grid=(t // ROUTE_TM, PEER_HEADS),
        in_specs=[pl.BlockSpec((ROUTE_TM, D_MODEL), lambda i, h: (i, 0)),
                  pl.BlockSpec((1, D_MODEL), lambda i, h: (0, 0)),
                  pl.BlockSpec((D_MODEL, PEER_QDIM), lambda i, h: (0, h)),
                  pl.BlockSpec((PEER_N_KEYS, PEER_QDIM // 2), lambda i, h: (0, 0)),
                  pl.BlockSpec((PEER_N_KEYS, PEER_QDIM // 2), lambda i, h: (0, 0)),
                  pl.BlockSpec((CAND_ROWS, LANES), lambda i, h: (0, 0))],
        out_specs=[pl.BlockSpec((ROUTE_TM, D_MODEL), lambda i, h: (i, 0)),
                   pl.BlockSpec((PEER_TOPK, ROUTE_TM), lambda i, h: (h, i)),
                   pl.BlockSpec((PEER_TOPK, ROUTE_TM), lambda i, h: (h, i))],
        out_shape=[jax.ShapeDtypeStruct((t, D_MODEL), jnp.float32),
                   jax.ShapeDtypeStruct((PEER_PICKS, t), jnp.int32),
                   jax.ShapeDtypeStruct((PEER_PICKS, t), jnp.float32)],
        scratch_shapes=[pltpu.VMEM((ROUTE_TM, D_MODEL), jnp.bfloat16)],
        compiler_params=pltpu.CompilerParams(dimension_semantics=("parallel", "arbitrary"),
                                             vmem_limit_bytes=VMEM_LIMIT_BYTES),
        name="peer_route",
    )(x, gain.reshape(1, D_MODEL), wq.astype(jnp.bfloat16), k1.astype(jnp.bfloat16), k2.astype(jnp.bfloat16),
      _cand_flat_ids())
    return xn, idx_t.T, gate_t.T


def _encoder_layer(x, p):
    b, l, _ = x.shape
    t = b * l
    xt = x.reshape(t, D_MODEL)
    w_in, gain = p['w_in'], p['norm1_w']
    ssm_w, swa_w, dil_w = sum(IN_SPLITS[:6]), sum(IN_SPLITS[6:9]), sum(IN_SPLITS[9:12])
    ssm_proj = _matmul(xt, w_in[:, :ssm_w], gain=gain, tn=384, keep_pad=True)
    swa_proj = _matmul(xt, w_in[:, ssm_w:ssm_w + swa_w], gain=gain, tn=768)
    dil_proj = _matmul(xt, w_in[:, ssm_w + swa_w:ssm_w + swa_w + dil_w], gain=gain, tn=768)
    gate_logits = _matmul(xt, w_in[:, ssm_w + swa_w + dil_w:], gain=gain, tn=1024)
    y_a = _ssd_mixer(ssm_proj, p, b, l)
    cos, sin = _rope_tables(l)
    y_b = _window_attention(swa_proj, cos, sin, p['q_norm_b'], p['k_norm_b'], p['sink_b'], b, l)
    y_c = _dilated_attention(dil_proj, cos, sin, p['q_norm_c'], p['k_norm_c'], b, l)
    xt = _mixer_out(xt, y_a, y_b, y_c, gate_logits, p['w_a'], p['w_b'], p['w_c'], p['w_out'])
    xn, idx, gate = _peer_route(xt, p['norm2_w'], p['peer_wq'], p['peer_k1'], p['peer_k2'])
    xt = _peer_experts(xt, xn, idx, gate, p['u_packed'], p['v_packed'])
    return xt.reshape(b, l, D_MODEL)


_LAYER_KEYS = ('norm1_w', 'w_in', 'conv_w', 'conv_b', 'dt_bias_f', 'dt_bias_b', 'a_log_f', 'a_log_b', 'd_skip',
               'ssm_norm_w', 'w_a', 'q_norm_b', 'k_norm_b', 'sink_b', 'w_b', 'q_norm_c', 'k_norm_c', 'w_c', 'w_out',
               'norm2_w', 'peer_wq', 'peer_k1', 'peer_k2', 'peer_u', 'peer_v')


def kernel(x_prompt, x_sample, norm1_w, w_in, conv_w, conv_b, dt_bias_f, dt_bias_b, a_log_f, a_log_b, d_skip, ssm_norm_w, w_a, q_norm_b, k_norm_b, sink_b, w_b, q_norm_c, k_norm_c, w_c, w_out, norm2_w, peer_wq, peer_k1, peer_k2, peer_u, peer_v):
    stacked = dict(zip(_LAYER_KEYS, (norm1_w, w_in, conv_w, conv_b, dt_bias_f, dt_bias_b, a_log_f, a_log_b, d_skip,
                                     ssm_norm_w, w_a, q_norm_b, k_norm_b, sink_b, w_b, q_norm_c, k_norm_c, w_c, w_out,
                                     norm2_w, peer_wq, peer_k1, peer_k2, peer_u, peer_v)))
    layers = []
    for i in range(w_in.shape[0]):
        p = {k: v[i] for k, v in stacked.items()}
        p['u_packed'] = _pack_table(p['peer_u'])
        p['v_packed'] = _pack_table(p['peer_v'])
        layers.append(p)

    def trunk(x):
        for p in layers:
            x = _encoder_layer(x, p)
        return x

    return trunk(x_prompt), trunk(x_sample)
```

```python
import functools
import math

import jax
import jax.numpy as jnp
from jax import lax
from jax.experimental import pallas as pl
from jax.experimental.pallas import tpu as pltpu

D_MODEL = 1024
HEAD_DIM = 64
SSM_HEADS = 16
SSM_HEAD_DIM = 64
SSM_INNER = SSM_HEADS * SSM_HEAD_DIM
SSM_GROUPS = 2
SSM_STATE = 128
SSM_CHUNK = 128
CONV_WIDTH = 5
CONV_CH = SSM_INNER + 2 * SSM_GROUPS * SSM_STATE
SWA_HEADS = 16
SWA_KV_HEADS = 4
SWA_WINDOW = 128
SWA_BLOCK = 128
DIL_PATTERNS = ((128, 1), (512, 4), (2048, 16))
DIL_HEADS_PER_GROUP = 4
DIL_HEADS = DIL_HEADS_PER_GROUP * 3
DIL_OUT = DIL_HEADS_PER_GROUP * HEAD_DIM
PEER_HEADS = 8
PEER_N_KEYS = 128
PEER_EXPERTS = PEER_N_KEYS * PEER_N_KEYS
PEER_TOPK = 16
PEER_QDIM = 256
PEER_PICKS = PEER_HEADS * PEER_TOPK
N_BRANCHES = 3
ROPE_THETA = 10000.0
EPS = 1e-6
NEG_INF = -1e30
IN_SPLITS = (SSM_INNER, SSM_INNER, SSM_GROUPS * SSM_STATE, SSM_GROUPS * SSM_STATE, SSM_HEADS, SSM_HEADS,
             SWA_HEADS * HEAD_DIM, SWA_KV_HEADS * HEAD_DIM, SWA_KV_HEADS * HEAD_DIM,
             DIL_HEADS * HEAD_DIM, DIL_HEADS * HEAD_DIM, DIL_HEADS * HEAD_DIM,
             N_BRANCHES * D_MODEL)
IN_COLS = sum(IN_SPLITS)

LANES = 128
SUBLANES = 8
VMEM_LIMIT_BYTES = 56 * 1024 * 1024

PACK_WORDS = D_MODEL // 2
PACK_ROWS = PACK_WORDS // LANES
ROW_SUBLANES = D_MODEL // LANES
PEER_TB = 32


def _mm_kernel(x_ref, g_ref, w_ref, o_ref, xn_ref, *, norm):
    @pl.when(pl.program_id(1) == 0)
    def _():
        x = x_ref[...].astype(jnp.float32)
        if norm:
            x = x * lax.rsqrt(jnp.mean(x * x, axis=-1, keepdims=True) + EPS) * g_ref[...]
        xn_ref[...] = x.astype(jnp.bfloat16)

    o_ref[...] = jnp.dot(xn_ref[...], w_ref[...], preferred_element_type=jnp.float32)


def _matmul(x, w, gain=None, tm=1024, tn=512, keep_pad=False):
    m, k = x.shape
    n = w.shape[1]
    tm = min(tm, m)
    tn = min(tn, n)
    assert m % tm == 0
    n_pad = -n % tn
    wb = w.astype(jnp.bfloat16)
    if n_pad:
        wb = jnp.pad(wb, ((0, 0), (0, n_pad)))
    g = jnp.ones((1, k), jnp.float32) if gain is None else gain.reshape(1, k).astype(jnp.float32)
    out = pl.pallas_call(
        functools.partial(_mm_kernel, norm=gain is not None),
        grid=(m // tm, (n + n_pad) // tn),
        in_specs=[pl.BlockSpec((tm, k), lambda i, j: (i, 0)),
                  pl.BlockSpec((1, k), lambda i, j: (0, 0)),
                  pl.BlockSpec((k, tn), lambda i, j: (0, j))],
        out_specs=pl.BlockSpec((tm, tn), lambda i, j: (i, j)),
        out_shape=jax.ShapeDtypeStruct((m, n + n_pad), jnp.float32),
        scratch_shapes=[pltpu.VMEM((tm, k), jnp.bfloat16)],
        compiler_params=pltpu.CompilerParams(dimension_semantics=("parallel", "arbitrary"),
                                             vmem_limit_bytes=VMEM_LIMIT_BYTES),
        name="proj_matmul",
    )(x, g, wb)
    return out[:, :n] if n_pad and not keep_pad else out


def _mixer_out_kernel(x_ref, ya_ref, yb_ref, yc_ref, gl_ref, wa_ref, wb_ref, wc_ref, wo_ref, o_ref):
    def proj(y_ref, w_ref):
        return jnp.dot(y_ref[...].astype(jnp.bfloat16), w_ref[...], preferred_element_type=jnp.float32)

    gates = jax.nn.sigmoid(gl_ref[...])
    merged = (gates[:, :D_MODEL] * proj(ya_ref, wa_ref) + gates[:, D_MODEL:2 * D_MODEL] * proj(yb_ref, wb_ref)
              + gates[:, 2 * D_MODEL:] * proj(yc_ref, wc_ref))
    o_ref[...] = x_ref[...] + jnp.dot(merged.astype(jnp.bfloat16), wo_ref[...], preferred_element_type=jnp.float32)


def _mixer_out(x, ya, yb, yc, gate_logits, w_a, w_b, w_c, w_out, tm=512):
    t = x.shape[0]
    assert t % tm == 0
    ws = [w.astype(jnp.bfloat16) for w in (w_a, w_b, w_c, w_out)]
    row = lambda a: pl.BlockSpec((tm, a.shape[1]), lambda i: (i, 0))
    full = lambda a: pl.BlockSpec(a.shape, lambda i: (0, 0))
    acts = (x, ya, yb, yc, gate_logits)
    return pl.pallas_call(
        _mixer_out_kernel,
        grid=(t // tm,),
        in_specs=[row(a) for a in acts] + [full(w) for w in ws],
        out_specs=row(x),
        out_shape=jax.ShapeDtypeStruct(x.shape, jnp.float32),
        compiler_params=pltpu.CompilerParams(dimension_semantics=("parallel",), vmem_limit_bytes=VMEM_LIMIT_BYTES),
        name="mixer_out",
    )(*acts, *ws)


def _pack_table(tab):
    e = tab.shape[0]
    tb = lax.bitcast_convert_type(tab.astype(jnp.bfloat16), jnp.uint16).astype(jnp.uint32)
    tb = tb.reshape(e, PACK_ROWS, 2, LANES)
    packed = tb[:, :, 0, :] | (tb[:, :, 1, :] << 16)
    return lax.bitcast_convert_type(packed, jnp.int32).reshape(e * PACK_ROWS, LANES)


def _peer_constants():
    col = jnp.arange(PEER_PICKS * ROW_SUBLANES)
    diag = (col[None, :] % ROW_SUBLANES == jnp.arange(ROW_SUBLANES)[:, None]).astype(jnp.float32)
    group = (col[:, None] // ROW_SUBLANES == jnp.arange(PEER_PICKS)[None, :]).astype(jnp.bfloat16)
    return diag, group, group.T


def _split2(x):
    hi = x.astype(jnp.bfloat16).astype(jnp.float32)
    return hi, x - hi


def _stack_pieces(pieces):
    return jnp.concatenate(list(pieces), axis=1).astype(jnp.bfloat16)


def _sum_pieces(y):
    return y[:ROW_SUBLANES] + y[ROW_SUBLANES:]


def _gather_rows(idx_ref, tab_ref, t, raw_ref):
    for j in range(PEER_PICKS):
        src = pl.ds(pl.multiple_of(idx_ref[t, j], PACK_ROWS), PACK_ROWS)
        raw_ref[pl.ds(j * PACK_ROWS, PACK_ROWS), :] = tab_ref[src, :]


def _token_rows(t):
    return pl.ds(t * ROW_SUBLANES, ROW_SUBLANES)


def _pipelined_tokens(gather, compute, bufs):
    gather(0, bufs[0])
    for t in range(PEER_TB):
        compute(t, bufs[t % 2])
        if t + 1 < PEER_TB:
            gather(t + 1, bufs[(t + 1) % 2])


def _with_staged_ids(idx_hbm, idx_smem, sem, body):
    i = pl.program_id(0)
    n = pl.num_programs(0)

    def copy(step, slot):
        return pltpu.make_async_copy(idx_hbm.at[pl.ds(step * PEER_TB, PEER_TB)], idx_smem.at[slot], sem.at[slot])

    @pl.when(i == 0)
    def _():
        copy(0, 0).start()

    for slot in range(2):
        @pl.when(i % 2 == slot)
        def _():
            @pl.when(i + 1 < n)
            def _():
                copy(i + 1, 1 - slot).start()

            copy(i, slot).wait()
            body(idx_smem.at[slot])


def _peer_act_kernel(idx_hbm, x_ref, gate_ref, diag_ref, group_ref, tab_ref, w_ref, raw0_ref, raw1_ref, x3_ref,
                     picked_ref, idx_smem, sem):
    nt = (((1,), (1,)), ((), ()))
    x3_ref[...] = _stack_pieces(_split2(x_ref[...].reshape(PEER_TB, ROW_SUBLANES, LANES)))

    def compute(t, buf_ref):
        wmat = pltpu.bitcast(buf_ref[...], jnp.bfloat16)
        scores = _sum_pieces(lax.dot_general(x3_ref[t], wmat, nt, preferred_element_type=jnp.float32))
        picked_ref[_token_rows(t), :] = scores * diag_ref[...]

    _with_staged_ids(idx_hbm, idx_smem, sem, lambda ids: _pipelined_tokens(
        functools.partial(_gather_rows, ids, tab_ref), compute, (raw0_ref, raw1_ref)))

    act8 = sum(jnp.dot(piece.astype(jnp.bfloat16), group_ref[...], preferred_element_type=jnp.float32)
               for piece in _split2(picked_ref[...]))
    act = jnp.sum(act8.reshape(PEER_TB, ROW_SUBLANES, PEER_PICKS), axis=1)
    gelu = 0.5 * act * (1.0 + lax.erf(act * (2.0 ** -0.5)))
    w_ref[...] = gate_ref[...] * gelu


def _peer_out_kernel(idx_hbm, w_ref, res_ref, diag_ref, expand_ref, tab_ref, o_ref, raw0_ref, raw1_ref, lhs_ref,
                     idx_smem, sem):
    wide = [jnp.dot(piece.astype(jnp.bfloat16), expand_ref[...], preferred_element_type=jnp.float32)
            for piece in _split2(w_ref[...])]
    diag = diag_ref[...]
    for t in range(PEER_TB):
        rows = [jnp.broadcast_to(wd[t:t + 1], diag.shape) * diag for wd in wide]
        lhs_ref[t] = jnp.concatenate(rows, axis=0).astype(jnp.bfloat16)

    def compute(t, buf_ref):
        wmat = pltpu.bitcast(buf_ref[...], jnp.bfloat16)
        rows = _token_rows(t)
        o_ref[rows, :] = res_ref[rows, :] + _sum_pieces(jnp.dot(lhs_ref[t], wmat, preferred_element_type=jnp.float32))

    _with_staged_ids(idx_hbm, idx_smem, sem, lambda ids: _pipelined_tokens(
        functools.partial(_gather_rows, ids, tab_ref), compute, (raw0_ref, raw1_ref)))


def _const_spec(shape):
    return pl.BlockSpec(shape, lambda i: (0,) * len(shape), pipeline_mode=pl.Buffered(1))


def _peer_experts(x_res, xn, idx, gate, u_packed, v_packed):
    t = xn.shape[0]
    assert t % PEER_TB == 0 and PEER_TB % 2 == 0
    grid = (t // PEER_TB,)
    params = pltpu.CompilerParams(dimension_semantics=("arbitrary",), vmem_limit_bytes=VMEM_LIMIT_BYTES)
    ids_spec = pl.BlockSpec(memory_space=pl.ANY)
    ids_scratch = [pltpu.SMEM((2, PEER_TB, PEER_PICKS), jnp.int32), pltpu.SemaphoreType.DMA((2,))]
    rows_spec = pl.BlockSpec((PEER_TB * ROW_SUBLANES, LANES), lambda i: (i, 0))
    picks_spec = pl.BlockSpec((PEER_TB, PEER_PICKS), lambda i: (i, 0))
    raw = pltpu.VMEM((PEER_PICKS * PACK_ROWS, LANES), jnp.int32)
    diag, group, expand = _peer_constants()
    stacked = 2 * ROW_SUBLANES

    w = pl.pallas_call(
        _peer_act_kernel,
        grid=grid,
        in_specs=[ids_spec, rows_spec, picks_spec, _const_spec(diag.shape), _const_spec(group.shape),
                  _const_spec(u_packed.shape)],
        out_specs=picks_spec,
        out_shape=jax.ShapeDtypeStruct((t, PEER_PICKS), jnp.float32),
        scratch_shapes=[raw, raw, pltpu.VMEM((PEER_TB, stacked, LANES), jnp.bfloat16),
                        pltpu.VMEM((PEER_TB * ROW_SUBLANES, PEER_PICKS * ROW_SUBLANES), jnp.float32)] + ids_scratch,
        compiler_params=params,
        name="peer_act",
    )(idx, xn.reshape(t * ROW_SUBLANES, LANES), gate, diag, group, u_packed)

    out = pl.pallas_call(
        _peer_out_kernel,
        grid=grid,
        in_specs=[ids_spec, picks_spec, rows_spec, _const_spec(diag.shape), _const_spec(expand.shape),
                  _const_spec(v_packed.shape)],
        out_specs=rows_spec,
        out_shape=jax.ShapeDtypeStruct((t * ROW_SUBLANES, LANES), jnp.float32),
        scratch_shapes=[raw, raw, pltpu.VMEM((PEER_TB, stacked, PEER_PICKS * ROW_SUBLANES), jnp.bfloat16)]
        + ids_scratch,
        compiler_params=params,
        name="peer_out",
    )(idx, w, x_res.reshape(t * ROW_SUBLANES, LANES), diag, expand, v_packed)
    return out.reshape(t, D_MODEL)


ATT_BLK = 128
HALF_HEAD = HEAD_DIM // 2


def _rope_tables(seq):
    inv_freq = ROPE_THETA ** (-jnp.arange(HALF_HEAD, dtype=jnp.float32) / HALF_HEAD)
    ang = jnp.arange(seq).astype(jnp.float32)[:, None] * inv_freq[None, :]
    cos, sin = jnp.cos(ang), jnp.sin(ang)
    return jnp.tile(jnp.concatenate([cos, cos], axis=1), (1, 2)), jnp.tile(jnp.concatenate([-sin, sin], axis=1), (1, 2))


def _norm_rope(a, gain, cos, sin):
    lane = lax.broadcasted_iota(jnp.int32, (a.shape[0], LANES), 1)
    low_head = lane < HEAD_DIM
    first_half = (lane % HEAD_DIM) < HALF_HEAD
    out = []
    for c in range(a.shape[1] // LANES):
        x = a[:, c * LANES:(c + 1) * LANES]
        sq = x * x
        lo = jnp.sum(jnp.where(low_head, sq, 0.0), axis=1, keepdims=True)
        hi = jnp.sum(jnp.where(low_head, 0.0, sq), axis=1, keepdims=True)
        y = x * lax.rsqrt(jnp.where(low_head, lo, hi) * (1.0 / HEAD_DIM) + EPS) * gain
        rot = jnp.where(first_half, pltpu.roll(y, LANES - HALF_HEAD, axis=1), pltpu.roll(y, HALF_HEAD, axis=1))
        out.append(y * cos + rot * sin)
    return out


def _attn_kernel(sink_ref, q_ref, kp_ref, kc_ref, kn_ref, vp_ref, vc_ref, vn_ref, cp_ref, cc_ref, cn_ref,
                 sp_ref, sc_ref, sn_ref, qg_ref, kg_ref, o_ref, *lse_refs, hq, hkv, window, use_sink):
    n = pl.program_id(2)
    nblk = pl.num_programs(2)
    nt = (((1,), (1,)), ((), ()))
    scale = HEAD_DIM ** -0.5
    q_chunks = _norm_rope(q_ref[...], qg_ref[...], cc_ref[...], sc_ref[...])
    k_parts = [_norm_rope(k_ref[...], kg_ref[...], c_ref[...], s_ref[...])
               for k_ref, c_ref, s_ref in ((kp_ref, cp_ref, sp_ref), (kc_ref, cc_ref, sc_ref), (kn_ref, cn_ref, sn_ref))]
    k_chunks = [jnp.concatenate([part[c] for part in k_parts], axis=0).astype(jnp.bfloat16)
                for c in range(hkv // 2)]
    v_all = jnp.concatenate([vp_ref[...], vc_ref[...], vn_ref[...]], axis=0).astype(jnp.bfloat16)

    row = lax.broadcasted_iota(jnp.int32, (ATT_BLK, 3 * ATT_BLK), 0)
    col = lax.broadcasted_iota(jnp.int32, (ATT_BLK, 3 * ATT_BLK), 1)
    rel = col - ATT_BLK - row
    first_col = jnp.where(n > 0, 0, ATT_BLK)
    end_col = jnp.where(n < nblk - 1, 3 * ATT_BLK, 2 * ATT_BLK)
    mask = (jnp.abs(rel) <= window) & (col >= first_col) & (col < end_col)
    lane = lax.broadcasted_iota(jnp.int32, (ATT_BLK, LANES), 1)
    halves = (lane < HEAD_DIM, lane >= HEAD_DIM)

    group = hq // hkv
    outs = [jnp.zeros((ATT_BLK, LANES), jnp.float32) for _ in range(hq // 2)]
    lses = [jnp.zeros((ATT_BLK, LANES), jnp.float32) for _ in range(hq // 2)]
    for h in range(hq):
        g = h // group
        qh = q_chunks[h // 2]
        if h % 2 != g % 2:
            qh = pltpu.roll(qh, HEAD_DIM, axis=1)
        qh = (jnp.where(halves[g % 2], qh, 0.0)).astype(jnp.bfloat16)
        s = lax.dot_general(qh, k_chunks[g // 2], nt, preferred_element_type=jnp.float32) * scale
        s = jnp.where(mask, s, NEG_INF)
        m = jnp.max(s, axis=1, keepdims=True)
        if use_sink:
            m = jnp.maximum(m, sink_ref[h])
        p = jnp.exp(s - m)
        den = jnp.sum(p, axis=1, keepdims=True)
        if use_sink:
            den = den + jnp.exp(sink_ref[h] - m)
        vg = v_all[:, (g // 2) * LANES:(g // 2 + 1) * LANES]
        o = jnp.dot(p.astype(jnp.bfloat16), vg, preferred_element_type=jnp.float32) / den
        if h % 2 != g % 2:
            o = pltpu.roll(o, HEAD_DIM, axis=1)
        outs[h // 2] = jnp.where(halves[h % 2], o, outs[h // 2])
        if lse_refs:
            lses[h // 2] = jnp.where(halves[h % 2], m + jnp.log(den), lses[h // 2])
    o_ref[...] = jnp.concatenate(outs, axis=1)
    if lse_refs:
        lse_refs[0][...] = jnp.concatenate(lses, axis=1)


def _banded_attention(src, cos, sin, q_gain, k_gain, sink, *, batch, seq, dil, hq, hkv, q_off, k_off, v_off, window,
                      want_lse):
    rows = seq // dil
    assert rows % ATT_BLK == 0 and src.shape[0] == batch * rows and src.shape[1] % dil == 0
    nblk = rows // ATT_BLK
    width = src.shape[1] // dil
    qw, kw = hq * HEAD_DIM, hkv * HEAD_DIM
    assert all((r * width + off) % w == 0 for r in range(dil) for off, w in ((q_off, qw), (k_off, kw), (v_off, kw)))

    def spec(w, off, shift):
        return pl.BlockSpec((ATT_BLK, w), lambda b, r, n: (b * nblk + jnp.clip(n + shift, 0, nblk - 1),
                                                           (r * width + off) // w))

    def table_spec(shift):
        return pl.BlockSpec((ATT_BLK, LANES), lambda b, r, n: (jnp.clip(n + shift, 0, nblk - 1), r))

    tile2 = lambda gain: jnp.tile(gain.astype(jnp.float32), 2).reshape(1, LANES)
    gain_spec = pl.BlockSpec((1, LANES), lambda b, r, n: (0, 0))
    out_spec = pl.BlockSpec((ATT_BLK, qw), lambda b, r, n: (b * nblk + n, r))
    out_shape = jax.ShapeDtypeStruct((batch * rows, dil * qw), jnp.float32)
    use_sink = sink is not None
    sink_arr = sink.astype(jnp.float32) if use_sink else jnp.zeros((hq,), jnp.float32)
    cos_d, sin_d = cos.reshape(rows, dil * LANES), sin.reshape(rows, dil * LANES)
    res = pl.pallas_call(
        functools.partial(_attn_kernel, hq=hq, hkv=hkv, window=window, use_sink=use_sink),
        grid=(batch, dil, nblk),
        in_specs=[pl.BlockSpec(memory_space=pltpu.SMEM), spec(qw, q_off, 0),
                  spec(kw, k_off, -1), spec(kw, k_off, 0), spec(kw, k_off, 1),
                  spec(kw, v_off, -1), spec(kw, v_off, 0), spec(kw, v_off, 1),
                  table_spec(-1), table_spec(0), table_spec(1), table_spec(-1), table_spec(0), table_spec(1),
                  gain_spec, gain_spec],
        out_specs=[out_spec, out_spec] if want_lse else [out_spec],
        out_shape=[out_shape, out_shape] if want_lse else [out_shape],
        compiler_params=pltpu.CompilerParams(dimension_semantics=("parallel", "parallel", "arbitrary"),
                                             vmem_limit_bytes=VMEM_LIMIT_BYTES),
        name="banded_attention",
    )(sink_arr, src, src, src, src, src, src, src, cos_d, cos_d, cos_d, sin_d, sin_d, sin_d, tile2(q_gain), tile2(k_gain))
    return res if want_lse else res[0]


def _window_attention(proj, cos, sin, q_gain, k_gain, sink, batch, seq):
    qw = SWA_HEADS * HEAD_DIM
    kw = SWA_KV_HEADS * HEAD_DIM
    return _banded_attention(proj, cos, sin, q_gain, k_gain, sink, batch=batch, seq=seq, dil=1, hq=SWA_HEADS,
                             hkv=SWA_KV_HEADS, q_off=0, k_off=qw, v_off=qw + kw, window=SWA_WINDOW, want_lse=False)


def _merge_kernel(*refs):
    n = len(DIL_PATTERNS)
    o_refs, l_refs, out_ref = refs[:n], refs[n:2 * n], refs[2 * n]
    lses = [r[...] for r in l_refs]
    m = functools.reduce(jnp.maximum, lses)
    es = [jnp.exp(l - m) for l in lses]
    den = functools.reduce(lambda a, b: a + b, es)
    out_ref[...] = functools.reduce(lambda a, b: a + b, [(e / den) * o[...] for e, o in zip(es, o_refs)])


def _dilated_attention(proj, cos, sin, q_gain, k_gain, batch, seq):
    t = batch * seq
    gw = DIL_HEADS_PER_GROUP * HEAD_DIM
    full = DIL_HEADS * HEAD_DIM
    outs, lses = [], []
    for gi, (win, dil) in enumerate(DIL_PATTERNS):
        o, lse = _banded_attention(proj.reshape(t // dil, dil * proj.shape[1]), cos, sin, q_gain, k_gain, None,
                                   batch=batch, seq=seq, dil=dil, hq=DIL_HEADS_PER_GROUP, hkv=DIL_HEADS_PER_GROUP,
                                   q_off=gi * gw, k_off=full + gi * gw, v_off=2 * full + gi * gw,
                                   window=win // (2 * dil), want_lse=True)
        outs.append(o.reshape(t, gw))
        lses.append(lse.reshape(t, gw))
    tm = 1024
    spec = pl.BlockSpec((tm, gw), lambda i: (i, 0))
    return pl.pallas_call(
        _merge_kernel, grid=(t // tm,), in_specs=[spec] * (2 * len(DIL_PATTERNS)), out_specs=spec,
        out_shape=jax.ShapeDtypeStruct((t, gw), jnp.float32),
        compiler_params=pltpu.CompilerParams(dimension_semantics=("parallel",)), name="dilated_merge",
    )(*outs, *lses)


HALO = SUBLANES
DT_LANES = LANES


def _conv_silu(prev_ref, cur_ref, next_ref, w_ref, b_ref, has_prev, has_next):
    prev = jnp.where(has_prev, prev_ref[...], 0.0)
    nxt = jnp.where(has_next, next_ref[...], 0.0)
    cat = jnp.concatenate([prev, cur_ref[...], nxt], axis=0)
    w = w_ref[...]
    acc = b_ref[...]
    for k in range(CONV_WIDTH):
        start = HALO + k - CONV_WIDTH // 2
        acc = acc + cat[start:start + SSM_CHUNK, :] * w[k:k + 1, :]
    return acc * jax.nn.sigmoid(acc)


def _lane_col(a, j):
    return jnp.broadcast_to(a[:, j:j + 1], a.shape)


def _ssd_kernel(xp_ref, xc_ref, xn_ref, bp_ref, bc_ref, bn_ref, dt_ref, z_ref, part_ref, wx_ref, bx_ref, wb_ref,
                bb_ref, dtb_ref, aneg_ref, skip_ref, nw_ref, o_ref, h_ref, *, rev):
    c = pl.program_id(1)
    nchunk = pl.num_programs(1)
    first, last = c == 0, c == nchunk - 1
    has_prev = jnp.logical_not(last) if rev else jnp.logical_not(first)
    has_next = jnp.logical_not(first) if rev else jnp.logical_not(last)

    @pl.when(first)
    def _():
        h_ref[...] = jnp.zeros_like(h_ref)

    xs = _conv_silu(xp_ref, xc_ref, xn_ref, wx_ref, bx_ref, has_prev, has_next)
    bc = _conv_silu(bp_ref, bc_ref, bn_ref, wb_ref, bb_ref, has_prev, has_next)
    gn = SSM_GROUPS * SSM_STATE
    bm, cm = bc[:, :gn], bc[:, gn:]

    dt_raw = dt_ref[...]
    if rev:
        dt_raw = pltpu.roll(dt_raw, DT_LANES - SSM_HEADS, axis=1)
    xb = dt_raw + dtb_ref[...]
    dt = jnp.maximum(xb, 0.0) + jnp.log(1.0 + jnp.exp(-jnp.abs(xb)))
    da = dt * aneg_ref[...]

    row = lax.broadcasted_iota(jnp.int32, (SSM_CHUNK, SSM_CHUNK), 0)
    col = lax.broadcasted_iota(jnp.int32, (SSM_CHUNK, SSM_CHUNK), 1)
    keep = (col >= row) if rev else (col <= row)
    cs = jnp.dot(keep.astype(jnp.float32), da, precision=lax.Precision.HIGHEST,
                 preferred_element_type=jnp.float32)
    cs_t = cs.T
    end_row = 0 if rev else SSM_CHUNK - 1

    lane = lax.broadcasted_iota(jnp.int32, (SSM_CHUNK, LANES), 1)
    low = lane < SSM_HEAD_DIM
    nt = (((1,), (1,)), ((), ()))
    bf = jnp.bfloat16
    heads_per_group = SSM_HEADS // SSM_GROUPS
    y_chunks = []
    for pair in range(SSM_HEADS // 2):
        g = (2 * pair) // heads_per_group
        bm_g = bm[:, g * SSM_STATE:(g + 1) * SSM_STATE]
        cm_g = cm[:, g * SSM_STATE:(g + 1) * SSM_STATE]
        if pair % (heads_per_group // 2) == 0:
            cb = lax.dot_general(cm_g.astype(bf), bm_g.astype(bf), nt, preferred_element_type=jnp.float32)
            bm_t = bm_g.T.astype(bf)
        xs_c = xs[:, pair * LANES:(pair + 1) * LANES]
        per_head = []
        for j in (2 * pair, 2 * pair + 1):
            ccol = _lane_col(cs, j)
            dec = jnp.where(keep, jnp.exp(jnp.where(keep, ccol - cs_t[j:j + 1, :], 0.0)), 0.0)
            total = ccol[end_row:end_row + 1, :]
            per_head.append(((cb * dec).astype(bf), _lane_col(dt, j), jnp.exp(ccol), jnp.exp(total - ccol),
                             jnp.exp(total)))
        (m_a, d_a, e_a, t_a, g_a), (m_b, d_b, e_b, t_b, g_b) = per_head
        xdt = xs_c * jnp.where(low, d_a, d_b)
        xdt_b = xdt.astype(bf)
        y_diag = jnp.where(low, jnp.dot(m_a, xdt_b, preferred_element_type=jnp.float32),
                           jnp.dot(m_b, xdt_b, preferred_element_type=jnp.float32))
        h = h_ref[pair]
        y_off = jnp.dot(cm_g.astype(bf), h.astype(bf), preferred_element_type=jnp.float32) * jnp.where(low, e_a, e_b)
        states = jnp.dot(bm_t, (xdt * jnp.where(low, t_a, t_b)).astype(bf), preferred_element_type=jnp.float32)
        h_ref[pair] = h * jnp.where(low, g_a, g_b) + states
        y = y_diag + y_off
        if not rev:
            y = y + xs_c * skip_ref[:, pair * LANES:(pair + 1) * LANES]
        y_chunks.append(y)
    y = jnp.concatenate(y_chunks, axis=1)
    if not rev:
        o_ref[...] = y
    else:
        z = z_ref[...]
        y = (y + part_ref[...]) * (z * jax.nn.sigmoid(z))
        o_ref[...] = y * lax.rsqrt(jnp.mean(y * y, axis=-1, keepdims=True) + EPS) * nw_ref[...]


def _ssd_mixer(proj, p, batch, seq):
    nchunk = seq // SSM_CHUNK
    assert seq % SSM_CHUNK == 0
    f32 = jnp.float32
    halo_per_chunk = SSM_CHUNK // HALO
    xcol = 1
    bc_w = 2 * SSM_GROUPS * SSM_STATE
    bc_col = (2 * SSM_INNER) // bc_w
    dt_col = (2 * SSM_INNER + bc_w) // DT_LANES
    pad16 = lambda a: jnp.pad(a.astype(f32), (0, LANES - SSM_HEADS)).reshape(1, LANES)
    per_lane = lambda a: jnp.repeat(a.astype(f32), SSM_HEAD_DIM).reshape(1, SSM_INNER)
    wx, wb = p['conv_w'][:, :SSM_INNER].astype(f32), p['conv_w'][:, SSM_INNER:].astype(f32)
    bx, bb = p['conv_b'][:SSM_INNER].reshape(1, -1).astype(f32), p['conv_b'][SSM_INNER:].reshape(1, -1).astype(f32)
    skip = per_lane(p['d_skip'])
    nw = p['ssm_norm_w'].reshape(1, SSM_INNER).astype(f32)

    def run(rev, part):
        pos = (lambda c: nchunk - 1 - c) if rev else (lambda c: c)

        def chunk(w, colblk):
            return pl.BlockSpec((SSM_CHUNK, w), lambda b, c: (b * nchunk + pos(c), colblk))

        def halo(w, colblk, side):
            def index(b, c):
                r = (b * nchunk + pos(c)) * halo_per_chunk
                r = r - 1 if side < 0 else r + halo_per_chunk
                return (jnp.clip(r, 0, batch * nchunk * halo_per_chunk - 1), colblk)
            return pl.BlockSpec((HALO, w), index)

        const = lambda a: pl.BlockSpec(a.shape, lambda b, c: (0, 0))
        dtb = pad16(p['dt_bias_b'] if rev else p['dt_bias_f'])
        aneg = pad16(-jnp.exp((p['a_log_b'] if rev else p['a_log_f']).astype(f32)))
        consts = (wx, bx, wb, bb, dtb, aneg, skip, nw)
        return pl.pallas_call(
            functools.partial(_ssd_kernel, rev=rev),
            grid=(batch, nchunk),
            in_specs=[halo(SSM_INNER, xcol, -1), chunk(SSM_INNER, xcol), halo(SSM_INNER, xcol, 1),
                      halo(bc_w, bc_col, -1), chunk(bc_w, bc_col), halo(bc_w, bc_col, 1),
                      chunk(DT_LANES, dt_col), chunk(SSM_INNER, 0), chunk(SSM_INNER, 0)] + [const(a) for a in consts],
            out_specs=chunk(SSM_INNER, 0),
            out_shape=jax.ShapeDtypeStruct((batch * seq, SSM_INNER), f32),
            scratch_shapes=[pltpu.VMEM((SSM_HEADS // 2, SSM_STATE, LANES), f32)],
            compiler_params=pltpu.CompilerParams(dimension_semantics=("parallel", "arbitrary"),
                                                 vmem_limit_bytes=VMEM_LIMIT_BYTES),
            name="ssd_bwd" if rev else "ssd_fwd",
        )(proj, proj, proj, proj, proj, proj, proj, proj, part, *consts)

    return run(True, run(False, proj))


ROUTE_TM = 256
CAND_GROUPS = ((0, 16), (1, 8), (2, 8), (3, 8), (4, 8), (5, 8), (6, 8), (7, 8))
CAND_ROWS = sum(n for _, n in CAND_GROUPS) + SUBLANES
NO_PAIR = PEER_TOPK * PEER_TOPK
TOPK_SHIFT = PEER_TOPK.bit_length() - 1
assert 1 << TOPK_SHIFT == PEER_TOPK


def _cand_flat_ids():
    flat = []
    for a, n in CAND_GROUPS:
        flat += [a * PEER_TOPK + b if (a + 1) * (b + 1) <= PEER_TOPK else NO_PAIR for b in range(n)]
    flat += [a * PEER_TOPK for a in range(SUBLANES, PEER_TOPK)]
    return jnp.broadcast_to(jnp.array(flat, jnp.int32)[:, None], (CAND_ROWS, LANES))


def _top_rows(s, row_ids, k, sentinel):
    slot = lax.broadcasted_iota(jnp.int32, (k, LANES), 0)
    vals = jnp.zeros((k, LANES), jnp.float32)
    ids = jnp.zeros((k, LANES), jnp.int32)
    for r in range(k):
        m = jnp.max(s, axis=0, keepdims=True)
        i = jnp.min(jnp.where(s == m, row_ids, sentinel), axis=0, keepdims=True)
        vals = jnp.where(slot == r, m, vals)
        ids = jnp.where(slot == r, i, ids)
        s = jnp.where(row_ids == i, -jnp.inf, s)
    return vals, ids


def _lookup_rows(table, pos):
    slot = lax.broadcasted_iota(jnp.int32, table.shape, 0)
    return jnp.sum(jnp.where(slot == pos, table, 0), axis=0, keepdims=True)


def _peer_route_kernel(x_ref, g_ref, wq_ref, k1_ref, k2_ref, flat_ref, xn_ref, idx_ref, gate_ref, xb_ref):
    nt = (((1,), (1,)), ((), ()))

    @pl.when(pl.program_id(1) == 0)
    def _():
        x = x_ref[...]
        xn = x * lax.rsqrt(jnp.mean(x * x, axis=-1, keepdims=True) + EPS) * g_ref[...]
        xn_ref[...] = xn
        xb_ref[...] = xn.astype(jnp.bfloat16)

    q = jnp.dot(xb_ref[...], wq_ref[...], preferred_element_type=jnp.float32).astype(jnp.bfloat16)
    half = PEER_QDIM // 2
    s1 = lax.dot_general(k1_ref[...], q[:, :half], nt, preferred_element_type=jnp.float32)
    s2 = lax.dot_general(k2_ref[...], q[:, half:], nt, preferred_element_type=jnp.float32)
    key_ids = lax.broadcasted_iota(jnp.int32, (PEER_N_KEYS, LANES), 0)
    flat = flat_ref[...]
    for lt in range(ROUTE_TM // LANES):
        lanes = slice(lt * LANES, (lt + 1) * LANES)
        v1, i1 = _top_rows(s1[:, lanes], key_ids, PEER_TOPK, PEER_N_KEYS)
        v2, i2 = _top_rows(s2[:, lanes], key_ids, PEER_TOPK, PEER_N_KEYS)
        blocks = [v1[a:a + 1] + v2[:n] for a, n in CAND_GROUPS] + [v1[SUBLANES:] + v2[0:1]]
        cand = jnp.where(flat < NO_PAIR, jnp.concatenate(blocks, axis=0), -jnp.inf)
        top_s, top_flat = _top_rows(cand, flat, PEER_TOPK, 2 * NO_PAIR)
        ids = jnp.concatenate(
            [_lookup_rows(i1, top_flat[r:r + 1] >> TOPK_SHIFT) * PEER_N_KEYS + _lookup_rows(i2, top_flat[r:r + 1] & (PEER_TOPK - 1))
             for r in range(PEER_TOPK)], axis=0)
        e = jnp.exp(top_s - top_s[0:1])
        idx_ref[:, lanes] = ids * PACK_ROWS
        gate_ref[:, lanes] = e / jnp.sum(e, axis=0, keepdims=True)


def _peer_route(x, gain, wq, k1, k2):
    t = x.shape[0]
    assert t % ROUTE_TM == 0
    xn, idx_t, gate_t = pl.pallas_call(
        _peer_route_kernel,
        grid=(t // ROUTE_TM, PEER_HEADS),
        in_specs=[pl.BlockSpec((ROUTE_TM, D_MODEL), lambda i, h: (i, 0)),
                  pl.BlockSpec((1, D_MODEL), lambda i, h: (0, 0)),
                  pl.BlockSpec((D_MODEL, PEER_QDIM), lambda i, h: (0, h)),
                  pl.BlockSpec((PEER_N_KEYS, PEER_QDIM // 2), lambda i, h: (0, 0)),
                  pl.BlockSpec((PEER_N_KEYS, PEER_QDIM // 2), lambda i, h: (0, 0)),
                  pl.BlockSpec((CAND_ROWS, LANES), lambda i, h: (0, 0))],
        out_specs=[pl.BlockSpec((ROUTE_TM, D_MODEL), lambda i, h: (i, 0)),
                   pl.BlockSpec((PEER_TOPK, ROUTE_TM), lambda i, h: (h, i)),
                   pl.BlockSpec((PEER_TOPK, ROUTE_TM), lambda i, h: (h, i))],
        out_shape=[jax.ShapeDtypeStruct((t, D_MODEL), jnp.float32),
                   jax.ShapeDtypeStruct((PEER_PICKS, t), jnp.int32),
                   jax.ShapeDtypeStruct((PEER_PICKS, t), jnp.float32)],
        scratch_shapes=[pltpu.VMEM((ROUTE_TM, D_MODEL), jnp.bfloat16)],
        compiler_params=pltpu.CompilerParams(dimension_semantics=("parallel", "arbitrary"),
                                             vmem_limit_bytes=VMEM_LIMIT_BYTES),
        name="peer_route",
    )(x, gain.reshape(1, D_MODEL), wq.astype(jnp.bfloat16), k1.astype(jnp.bfloat16), k2.astype(jnp.bfloat16),
      _cand_flat_ids())
    return xn, idx_t.T, gate_t.T


def _encoder_layer(x, p):
    b, l, _ = x.shape
    t = b * l
    xt = x.reshape(t, D_MODEL)
    w_in, gain = p['w_in'], p['norm1_w']
    ssm_w, swa_w, dil_w = sum(IN_SPLITS[:6]), sum(IN_SPLITS[6:9]), sum(IN_SPLITS[9:12])
    ssm_proj = _matmul(xt, w_in[:, :ssm_w], gain=gain, tn=384, keep_pad=True)
    swa_proj = _matmul(xt, w_in[:, ssm_w:ssm_w + swa_w], gain=gain, tn=768)
    dil_proj = _matmul(xt, w_in[:, ssm_w + swa_w:ssm_w + swa_w + dil_w], gain=gain, tn=768)
    gate_logits = _matmul(xt, w_in[:, ssm_w + swa_w + dil_w:], gain=gain, tn=1024)
    y_a = _ssd_mixer(ssm_proj, p, b, l)
    cos, sin = _rope_tables(l)
    y_b = _window_attention(swa_proj, cos, sin, p['q_norm_b'], p['k_norm_b'], p['sink_b'], b, l)
    y_c = _dilated_attention(dil_proj, cos, sin, p['q_norm_c'], p['k_norm_c'], b, l)
    xt = _mixer_out(xt, y_a, y_b, y_c, gate_logits, p['w_a'], p['w_b'], p['w_c'], p['w_out'])
    xn, idx, gate = _peer_route(xt, p['norm2_w'], p['peer_wq'], p['peer_k1'], p['peer_k2'])
    xt = _peer_experts(xt, xn, idx, gate, p['u_packed'], p['v_packed'])
    return xt.reshape(b, l, D_MODEL)


_LAYER_KEYS = ('norm1_w', 'w_in', 'conv_w', 'conv_b', 'dt_bias_f', 'dt_bias_b', 'a_log_f', 'a_log_b', 'd_skip',
               'ssm_norm_w', 'w_a', 'q_norm_b', 'k_norm_b', 'sink_b', 'w_b', 'q_norm_c', 'k_norm_c', 'w_c', 'w_out',
               'norm2_w', 'peer_wq', 'peer_k1', 'peer_k2', 'peer_u', 'peer_v')


def kernel(x_prompt, x_sample, norm1_w, w_in, conv_w, conv_b, dt_bias_f, dt_bias_b, a_log_f, a_log_b, d_skip, ssm_norm_w, w_a, q_norm_b, k_norm_b, sink_b, w_b, q_norm_c, k_norm_c, w_c, w_out, norm2_w, peer_wq, peer_k1, peer_k2, peer_u, peer_v):
    stacked = dict(zip(_LAYER_KEYS, (norm1_w, w_in, conv_w, conv_b, dt_bias_f, dt_bias_b, a_log_f, a_log_b, d_skip,
                                     ssm_norm_w, w_a, q_norm_b, k_norm_b, sink_b, w_b, q_norm_c, k_norm_c, w_c, w_out,
                                     norm2_w, peer_wq, peer_k1, peer_k2, peer_u, peer_v)))
    layers = []
    for i in range(w_in.shape[0]):
        p = {k: v[i] for k, v in stacked.items()}
        p['u_packed'] = _pack_table(p['peer_u'])
        p['v_packed'] = _pack_table(p['peer_v'])
        layers.append(p)

    def trunk(x):
        for p in layers:
            x = _encoder_layer(x, p)
        return x

    return trunk(x_prompt), trunk(x_sample)
```

```python
import functools
import math

import jax
import jax.numpy as jnp
from jax import lax
from jax.experimental import pallas as pl
from jax.experimental.pallas import tpu as pltpu

D_MODEL = 1024
HEAD_DIM = 64
SSM_HEADS = 16
SSM_HEAD_DIM = 64
SSM_INNER = SSM_HEADS * SSM_HEAD_DIM
SSM_GROUPS = 2
SSM_STATE = 128
SSM_CHUNK = 128
CONV_WIDTH = 5
CONV_CH = SSM_INNER + 2 * SSM_GROUPS * SSM_STATE
SWA_HEADS = 16
SWA_KV_HEADS = 4
SWA_WINDOW = 128
SWA_BLOCK = 128
DIL_PATTERNS = ((128, 1), (512, 4), (2048, 16))
DIL_HEADS_PER_GROUP = 4
DIL_HEADS = DIL_HEADS_PER_GROUP * 3
DIL_OUT = DIL_HEADS_PER_GROUP * HEAD_DIM
PEER_HEADS = 8
PEER_N_KEYS = 128
PEER_EXPERTS = PEER_N_KEYS * PEER_N_KEYS
PEER_TOPK = 16
PEER_QDIM = 256
PEER_PICKS = PEER_HEADS * PEER_TOPK
N_BRANCHES = 3
ROPE_THETA = 10000.0
EPS = 1e-6
NEG_INF = -1e30
IN_SPLITS = (SSM_INNER, SSM_INNER, SSM_GROUPS * SSM_STATE, SSM_GROUPS * SSM_STATE, SSM_HEADS, SSM_HEADS,
             SWA_HEADS * HEAD_DIM, SWA_KV_HEADS * HEAD_DIM, SWA_KV_HEADS * HEAD_DIM,
             DIL_HEADS * HEAD_DIM, DIL_HEADS * HEAD_DIM, DIL_HEADS * HEAD_DIM,
             N_BRANCHES * D_MODEL)
IN_COLS = sum(IN_SPLITS)

LANES = 128
SUBLANES = 8
VMEM_LIMIT_BYTES = 56 * 1024 * 1024

PACK_WORDS = D_MODEL // 2
PACK_ROWS = PACK_WORDS // LANES
ROW_SUBLANES = D_MODEL // LANES
PEER_TB = 32


def _mm_kernel(x_ref, g_ref, w_ref, o_ref, xn_ref, *, norm):
    @pl.when(pl.program_id(1) == 0)
    def _():
        x = x_ref[...].astype(jnp.float32)
        if norm:
            x = x * lax.rsqrt(jnp.mean(x * x, axis=-1, keepdims=True) + EPS) * g_ref[...]
        xn_ref[...] = x.astype(jnp.bfloat16)

    o_ref[...] = jnp.dot(xn_ref[...], w_ref[...], preferred_element_type=jnp.float32)


def _matmul(x, w, gain=None, tm=1024, tn=512, keep_pad=False):
    m, k = x.shape
    n = w.shape[1]
    tm = min(tm, m)
    tn = min(tn, n)
    assert m % tm == 0
    n_pad = -n % tn
    wb = w.astype(jnp.bfloat16)
    if n_pad:
        wb = jnp.pad(wb, ((0, 0), (0, n_pad)))
    g = jnp.ones((1, k), jnp.float32) if gain is None else gain.reshape(1, k).astype(jnp.float32)
    out = pl.pallas_call(
        functools.partial(_mm_kernel, norm=gain is not None),
        grid=(m // tm, (n + n_pad) // tn),
        in_specs=[pl.BlockSpec((tm, k), lambda i, j: (i, 0)),
                  pl.BlockSpec((1, k), lambda i, j: (0, 0)),
                  pl.BlockSpec((k, tn), lambda i, j: (0, j))],
        out_specs=pl.BlockSpec((tm, tn), lambda i, j: (i, j)),
        out_shape=jax.ShapeDtypeStruct((m, n + n_pad), jnp.float32),
        scratch_shapes=[pltpu.VMEM((tm, k), jnp.bfloat16)],
        compiler_params=pltpu.CompilerParams(dimension_semantics=("parallel", "arbitrary"),
                                             vmem_limit_bytes=VMEM_LIMIT_BYTES),
        name="proj_matmul",
    )(x, g, wb)
    return out[:, :n] if n_pad and not keep_pad else out


def _mixer_out_kernel(x_ref, ya_ref, yb_ref, yc_ref, gl_ref, wa_ref, wb_ref, wc_ref, wo_ref, o_ref):
    def proj(y_ref, w_ref):
        return jnp.dot(y_ref[...].astype(jnp.bfloat16), w_ref[...], preferred_element_type=jnp.float32)

    gates = jax.nn.sigmoid(gl_ref[...])
    merged = (gates[:, :D_MODEL] * proj(ya_ref, wa_ref) + gates[:, D_MODEL:2 * D_MODEL] * proj(yb_ref, wb_ref)
              + gates[:, 2 * D_MODEL:] * proj(yc_ref, wc_ref))
    o_ref[...] = x_ref[...] + jnp.dot(merged.astype(jnp.bfloat16), wo_ref[...], preferred_element_type=jnp.float32)


def _mixer_out(x, ya, yb, yc, gate_logits, w_a, w_b, w_c, w_out, tm=512):
    t = x.shape[0]
    assert t % tm == 0
    ws = [w.astype(jnp.bfloat16) for w in (w_a, w_b, w_c, w_out)]
    row = lambda a: pl.BlockSpec((tm, a.shape[1]), lambda i: (i, 0))
    full = lambda a: pl.BlockSpec(a.shape, lambda i: (0, 0))
    acts = (x, ya, yb, yc, gate_logits)
    return pl.pallas_call(
        _mixer_out_kernel,
        grid=(t // tm,),
        in_specs=[row(a) for a in acts] + [full(w) for w in ws],
        out_specs=row(x),
        out_shape=jax.ShapeDtypeStruct(x.shape, jnp.float32),
        compiler_params=pltpu.CompilerParams(dimension_semantics=("parallel",), vmem_limit_bytes=VMEM_LIMIT_BYTES),
        name="mixer_out",
    )(*acts, *ws)


def _pack_table(tab):
    e = tab.shape[0]
    tb = lax.bitcast_convert_type(tab.astype(jnp.bfloat16), jnp.uint16).astype(jnp.uint32)
    tb = tb.reshape(e, PACK_ROWS, 2, LANES)
    packed = tb[:, :, 0, :] | (tb[:, :, 1, :] << 16)
    return lax.bitcast_convert_type(packed, jnp.int32).reshape(e * PACK_ROWS, LANES)


def _peer_constants():
    col = jnp.arange(PEER_PICKS * ROW_SUBLANES)
    diag = (col[None, :] % ROW_SUBLANES == jnp.arange(ROW_SUBLANES)[:, None]).astype(jnp.float32)
    group = (col[:, None] // ROW_SUBLANES == jnp.arange(PEER_PICKS)[None, :]).astype(jnp.bfloat16)
    return diag, group, group.T


def _split2(x):
    hi = x.astype(jnp.bfloat16).astype(jnp.float32)
    return hi, x - hi


def _stack_pieces(pieces):
    return jnp.concatenate(list(pieces), axis=1).astype(jnp.bfloat16)


def _sum_pieces(y):
    return y[:ROW_SUBLANES] + y[ROW_SUBLANES:]


GATHER_BATCH = 16


def _gather_rows(idx_ref, tab_ref, t, raw_ref):
    for j0 in range(0, PEER_PICKS, GATHER_BATCH):
        rows = [tab_ref[pl.ds(pl.multiple_of(idx_ref[t, j], PACK_ROWS), PACK_ROWS), :]
                for j in range(j0, j0 + GATHER_BATCH)]
        for k in range(0, GATHER_BATCH, 2):
            raw_ref[pl.ds((j0 + k) * PACK_ROWS, 2 * PACK_ROWS), :] = jnp.concatenate(rows[k:k + 2], axis=0)


def _token_rows(t):
    return pl.ds(t * ROW_SUBLANES, ROW_SUBLANES)


def _pipelined_tokens(gather, compute, bufs):
    gather(0, bufs[0])
    for t in range(PEER_TB):
        compute(t, bufs[t % 2])
        if t + 1 < PEER_TB:
            gather(t + 1, bufs[(t + 1) % 2])


def _with_staged_ids(idx_hbm, idx_smem, sem, body):
    i = pl.program_id(0)
    n = pl.num_programs(0)

    def copy(step, slot):
        return pltpu.make_async_copy(idx_hbm.at[pl.ds(step * PEER_TB, PEER_TB)], idx_smem.at[slot], sem.at[slot])

    @pl.when(i == 0)
    def _():
        copy(0, 0).start()

    for slot in range(2):
        @pl.when(i % 2 == slot)
        def _():
            @pl.when(i + 1 < n)
            def _():
                copy(i + 1, 1 - slot).start()

            copy(i, slot).wait()
            body(idx_smem.at[slot])


def _peer_act_kernel(idx_hbm, x_ref, gate_ref, diag_ref, group_ref, tab_ref, w_ref, raw0_ref, raw1_ref, x3_ref,
                     picked_ref, idx_smem, sem):
    nt = (((1,), (1,)), ((), ()))
    x3_ref[...] = _stack_pieces(_split2(x_ref[...].reshape(PEER_TB, ROW_SUBLANES, LANES)))

    def compute(t, buf_ref):
        wmat = pltpu.bitcast(buf_ref[...], jnp.bfloat16)
        scores = _sum_pieces(lax.dot_general(x3_ref[t], wmat, nt, preferred_element_type=jnp.float32))
        picked_ref[_token_rows(t), :] = scores * diag_ref[...]

    _with_staged_ids(idx_hbm, idx_smem, sem, lambda ids: _pipelined_tokens(
        functools.partial(_gather_rows, ids, tab_ref), compute, (raw0_ref, raw1_ref)))

    act8 = sum(jnp.dot(piece.astype(jnp.bfloat16), group_ref[...], preferred_element_type=jnp.float32)
               for piece in _split2(picked_ref[...]))
    act = jnp.sum(act8.reshape(PEER_TB, ROW_SUBLANES, PEER_PICKS), axis=1)
    gelu = 0.5 * act * (1.0 + lax.erf(act * (2.0 ** -0.5)))
    w_ref[...] = gate_ref[...] * gelu


def _peer_out_kernel(idx_hbm, w_ref, res_ref, diag_ref, expand_ref, tab_ref, o_ref, raw0_ref, raw1_ref, lhs_ref,
                     idx_smem, sem):
    wide = [jnp.dot(piece.astype(jnp.bfloat16), expand_ref[...], preferred_element_type=jnp.float32)
            for piece in _split2(w_ref[...])]
    diag = diag_ref[...]
    for t in range(PEER_TB):
        rows = [jnp.broadcast_to(wd[t:t + 1], diag.shape) * diag for wd in wide]
        lhs_ref[t] = jnp.concatenate(rows, axis=0).astype(jnp.bfloat16)

    def compute(t, buf_ref):
        wmat = pltpu.bitcast(buf_ref[...], jnp.bfloat16)
        rows = _token_rows(t)
        o_ref[rows, :] = res_ref[rows, :] + _sum_pieces(jnp.dot(lhs_ref[t], wmat, preferred_element_type=jnp.float32))

    _with_staged_ids(idx_hbm, idx_smem, sem, lambda ids: _pipelined_tokens(
        functools.partial(_gather_rows, ids, tab_ref), compute, (raw0_ref, raw1_ref)))


def _const_spec(shape):
    return pl.BlockSpec(shape, lambda i: (0,) * len(shape), pipeline_mode=pl.Buffered(1))


def _peer_experts(x_res, xn, idx, gate, u_packed, v_packed):
    t = xn.shape[0]
    assert t % PEER_TB == 0 and PEER_TB % 2 == 0
    grid = (t // PEER_TB,)
    params = pltpu.CompilerParams(dimension_semantics=("arbitrary",), vmem_limit_bytes=VMEM_LIMIT_BYTES)
    ids_spec = pl.BlockSpec(memory_space=pl.ANY)
    ids_scratch = [pltpu.SMEM((2, PEER_TB, PEER_PICKS), jnp.int32), pltpu.SemaphoreType.DMA((2,))]
    rows_spec = pl.BlockSpec((PEER_TB * ROW_SUBLANES, LANES), lambda i: (i, 0))
    picks_spec = pl.BlockSpec((PEER_TB, PEER_PICKS), lambda i: (i, 0))
    raw = pltpu.VMEM((PEER_PICKS * PACK_ROWS, LANES), jnp.int32)
    diag, group, expand = _peer_constants()
    stacked = 2 * ROW_SUBLANES

    w = pl.pallas_call(
        _peer_act_kernel,
        grid=grid,
        in_specs=[ids_spec, rows_spec, picks_spec, _const_spec(diag.shape), _const_spec(group.shape),
                  _const_spec(u_packed.shape)],
        out_specs=picks_spec,
        out_shape=jax.ShapeDtypeStruct((t, PEER_PICKS), jnp.float32),
        scratch_shapes=[raw, raw, pltpu.VMEM((PEER_TB, stacked, LANES), jnp.bfloat16),
                        pltpu.VMEM((PEER_TB * ROW_SUBLANES, PEER_PICKS * ROW_SUBLANES), jnp.float32)] + ids_scratch,
        compiler_params=params,
        name="peer_act",
    )(idx, xn.reshape(t * ROW_SUBLANES, LANES), gate, diag, group, u_packed)

    out = pl.pallas_call(
        _peer_out_kernel,
        grid=grid,
        in_specs=[ids_spec, picks_spec, rows_spec, _const_spec(diag.shape), _const_spec(expand.shape),
                  _const_spec(v_packed.shape)],
        out_specs=rows_spec,
        out_shape=jax.ShapeDtypeStruct((t * ROW_SUBLANES, LANES), jnp.float32),
        scratch_shapes=[raw, raw, pltpu.VMEM((PEER_TB, stacked, PEER_PICKS * ROW_SUBLANES), jnp.bfloat16)]
        + ids_scratch,
        compiler_params=params,
        name="peer_out",
    )(idx, w, x_res.reshape(t * ROW_SUBLANES, LANES), diag, expand, v_packed)
    return out.reshape(t, D_MODEL)


ATT_BLK = 128
HALF_HEAD = HEAD_DIM // 2


def _rope_tables(seq):
    inv_freq = ROPE_THETA ** (-jnp.arange(HALF_HEAD, dtype=jnp.float32) / HALF_HEAD)
    ang = jnp.arange(seq).astype(jnp.float32)[:, None] * inv_freq[None, :]
    cos, sin = jnp.cos(ang), jnp.sin(ang)
    return jnp.tile(jnp.concatenate([cos, cos], axis=1), (1, 2)), jnp.tile(jnp.concatenate([-sin, sin], axis=1), (1, 2))


def _norm_rope(a, gain, cos, sin):
    lane = lax.broadcasted_iota(jnp.int32, (a.shape[0], LANES), 1)
    low_head = lane < HEAD_DIM
    first_half = (lane % HEAD_DIM) < HALF_HEAD
    out = []
    for c in range(a.shape[1] // LANES):
        x = a[:, c * LANES:(c + 1) * LANES]
        sq = x * x
        lo = jnp.sum(jnp.where(low_head, sq, 0.0), axis=1, keepdims=True)
        hi = jnp.sum(jnp.where(low_head, 0.0, sq), axis=1, keepdims=True)
        y = x * lax.rsqrt(jnp.where(low_head, lo, hi) * (1.0 / HEAD_DIM) + EPS) * gain
        rot = jnp.where(first_half, pltpu.roll(y, LANES - HALF_HEAD, axis=1), pltpu.roll(y, HALF_HEAD, axis=1))
        out.append(y * cos + rot * sin)
    return out


def _attn_kernel(sink_ref, q_ref, kp_ref, kc_ref, kn_ref, vp_ref, vc_ref, vn_ref, cp_ref, cc_ref, cn_ref,
                 sp_ref, sc_ref, sn_ref, qg_ref, kg_ref, o_ref, *lse_refs, hq, hkv, window, use_sink):
    n = pl.program_id(2)
    nblk = pl.num_programs(2)
    nt = (((1,), (1,)), ((), ()))
    scale = HEAD_DIM ** -0.5
    q_chunks = _norm_rope(q_ref[...], qg_ref[...], cc_ref[...], sc_ref[...])
    k_parts = [_norm_rope(k_ref[...], kg_ref[...], c_ref[...], s_ref[...])
               for k_ref, c_ref, s_ref in ((kp_ref, cp_ref, sp_ref), (kc_ref, cc_ref, sc_ref), (kn_ref, cn_ref, sn_ref))]
    k_chunks = [jnp.concatenate([part[c] for part in k_parts], axis=0).astype(jnp.bfloat16)
                for c in range(hkv // 2)]
    v_all = jnp.concatenate([vp_ref[...], vc_ref[...], vn_ref[...]], axis=0).astype(jnp.bfloat16)

    row = lax.broadcasted_iota(jnp.int32, (ATT_BLK, 3 * ATT_BLK), 0)
    col = lax.broadcasted_iota(jnp.int32, (ATT_BLK, 3 * ATT_BLK), 1)
    rel = col - ATT_BLK - row
    first_col = jnp.where(n > 0, 0, ATT_BLK)
    end_col = jnp.where(n < nblk - 1, 3 * ATT_BLK, 2 * ATT_BLK)
    mask = (jnp.abs(rel) <= window) & (col >= first_col) & (col < end_col)
    lane = lax.broadcasted_iota(jnp.int32, (ATT_BLK, LANES), 1)
    halves = (lane < HEAD_DIM, lane >= HEAD_DIM)

    group = hq // hkv
    outs = [jnp.zeros((ATT_BLK, LANES), jnp.float32) for _ in range(hq // 2)]
    lses = [jnp.zeros((ATT_BLK, LANES), jnp.float32) for _ in range(hq // 2)]
    for h in range(hq):
        g = h // group
        qh = q_chunks[h // 2]
        if h % 2 != g % 2:
            qh = pltpu.roll(qh, HEAD_DIM, axis=1)
        qh = (jnp.where(halves[g % 2], qh, 0.0)).astype(jnp.bfloat16)
        s = lax.dot_general(qh, k_chunks[g // 2], nt, preferred_element_type=jnp.float32) * scale
        s = jnp.where(mask, s, NEG_INF)
        m = jnp.max(s, axis=1, keepdims=True)
        if use_sink:
            m = jnp.maximum(m, sink_ref[h])
        p = jnp.exp(s - m)
        den = jnp.sum(p, axis=1, keepdims=True)
        if use_sink:
            den = den + jnp.exp(sink_ref[h] - m)
        vg = v_all[:, (g // 2) * LANES:(g // 2 + 1) * LANES]
        o = jnp.dot(p.astype(jnp.bfloat16), vg, preferred_element_type=jnp.float32) / den
        if h % 2 != g % 2:
            o = pltpu.roll(o, HEAD_DIM, axis=1)
        outs[h // 2] = jnp.where(halves[h % 2], o, outs[h // 2])
        if lse_refs:
            lses[h // 2] = jnp.where(halves[h % 2], m + jnp.log(den), lses[h // 2])
    o_ref[...] = jnp.concatenate(outs, axis=1)
    if lse_refs:
        lse_refs[0][...] = jnp.concatenate(lses, axis=1)


def _banded_attention(src, cos, sin, q_gain, k_gain, sink, *, batch, seq, dil, hq, hkv, q_off, k_off, v_off, window,
                      want_lse):
    rows = seq // dil
    assert rows % ATT_BLK == 0 and src.shape[0] == batch * rows and src.shape[1] % dil == 0
    nblk = rows // ATT_BLK
    width = src.shape[1] // dil
    qw, kw = hq * HEAD_DIM, hkv * HEAD_DIM
    assert all((r * width + off) % w == 0 for r in range(dil) for off, w in ((q_off, qw), (k_off, kw), (v_off, kw)))

    def spec(w, off, shift):
        return pl.BlockSpec((ATT_BLK, w), lambda b, r, n: (b * nblk + jnp.clip(n + shift, 0, nblk - 1),
                                                           (r * width + off) // w))

    def table_spec(shift):
        return pl.BlockSpec((ATT_BLK, LANES), lambda b, r, n: (jnp.clip(n + shift, 0, nblk - 1), r))

    tile2 = lambda gain: jnp.tile(gain.astype(jnp.float32), 2).reshape(1, LANES)
    gain_spec = pl.BlockSpec((1, LANES), lambda b, r, n: (0, 0))
    out_spec = pl.BlockSpec((ATT_BLK, qw), lambda b, r, n: (b * nblk + n, r))
    out_shape = jax.ShapeDtypeStruct((batch * rows, dil * qw), jnp.float32)
    use_sink = sink is not None
    sink_arr = sink.astype(jnp.float32) if use_sink else jnp.zeros((hq,), jnp.float32)
    cos_d, sin_d = cos.reshape(rows, dil * LANES), sin.reshape(rows, dil * LANES)
    res = pl.pallas_call(
        functools.partial(_attn_kernel, hq=hq, hkv=hkv, window=window, use_sink=use_sink),
        grid=(batch, dil, nblk),
        in_specs=[pl.BlockSpec(memory_space=pltpu.SMEM), spec(qw, q_off, 0),
                  spec(kw, k_off, -1), spec(kw, k_off, 0), spec(kw, k_off, 1),
                  spec(kw, v_off, -1), spec(kw, v_off, 0), spec(kw, v_off, 1),
                  table_spec(-1), table_spec(0), table_spec(1), table_spec(-1), table_spec(0), table_spec(1),
                  gain_spec, gain_spec],
        out_specs=[out_spec, out_spec] if want_lse else [out_spec],
        out_shape=[out_shape, out_shape] if want_lse else [out_shape],
        compiler_params=pltpu.CompilerParams(dimension_semantics=("parallel", "parallel", "arbitrary"),
                                             vmem_limit_bytes=VMEM_LIMIT_BYTES),
        name="banded_attention",
    )(sink_arr, src, src, src, src, src, src, src, cos_d, cos_d, cos_d, sin_d, sin_d, sin_d, tile2(q_gain), tile2(k_gain))
    return res if want_lse else res[0]


def _window_attention(proj, cos, sin, q_gain, k_gain, sink, batch, seq):
    qw = SWA_HEADS * HEAD_DIM
    kw = SWA_KV_HEADS * HEAD_DIM
    return _banded_attention(proj, cos, sin, q_gain, k_gain, sink, batch=batch, seq=seq, dil=1, hq=SWA_HEADS,
                             hkv=SWA_KV_HEADS, q_off=0, k_off=qw, v_off=qw + kw, window=SWA_WINDOW, want_lse=False)


def _merge_kernel(*refs):
    n = len(DIL_PATTERNS)
    o_refs, l_refs, out_ref = refs[:n], refs[n:2 * n], refs[2 * n]
    lses = [r[...] for r in l_refs]
    m = functools.reduce(jnp.maximum, lses)
    es = [jnp.exp(l - m) for l in lses]
    den = functools.reduce(lambda a, b: a + b, es)
    out_ref[...] = functools.reduce(lambda a, b: a + b, [(e / den) * o[...] for e, o in zip(es, o_refs)])


def _dilated_attention(proj, cos, sin, q_gain, k_gain, batch, seq):
    t = batch * seq
    gw = DIL_HEADS_PER_GROUP * HEAD_DIM
    full = DIL_HEADS * HEAD_DIM
    outs, lses = [], []
    for gi, (win, dil) in enumerate(DIL_PATTERNS):
        o, lse = _banded_attention(proj.reshape(t // dil, dil * proj.shape[1]), cos, sin, q_gain, k_gain, None,
                                   batch=batch, seq=seq, dil=dil, hq=DIL_HEADS_PER_GROUP, hkv=DIL_HEADS_PER_GROUP,
                                   q_off=gi * gw, k_off=full + gi * gw, v_off=2 * full + gi * gw,
                                   window=win // (2 * dil), want_lse=True)
        outs.append(o.reshape(t, gw))
        lses.append(lse.reshape(t, gw))
    tm = 1024
    spec = pl.BlockSpec((tm, gw), lambda i: (i, 0))
    return pl.pallas_call(
        _merge_kernel, grid=(t // tm,), in_specs=[spec] * (2 * len(DIL_PATTERNS)), out_specs=spec,
        out_shape=jax.ShapeDtypeStruct((t, gw), jnp.float32),
        compiler_params=pltpu.CompilerParams(dimension_semantics=("parallel",)), name="dilated_merge",
    )(*outs, *lses)


HALO = SUBLANES
DT_LANES = LANES


def _conv_silu(prev_ref, cur_ref, next_ref, w_ref, b_ref, has_prev, has_next):
    prev = jnp.where(has_prev, prev_ref[...], 0.0)
    nxt = jnp.where(has_next, next_ref[...], 0.0)
    cat = jnp.concatenate([prev, cur_ref[...], nxt], axis=0)
    w = w_ref[...]
    acc = b_ref[...]
    for k in range(CONV_WIDTH):
        start = HALO + k - CONV_WIDTH // 2
        acc = acc + cat[start:start + SSM_CHUNK, :] * w[k:k + 1, :]
    return acc * jax.nn.sigmoid(acc)


def _lane_col(a, j):
    return jnp.broadcast_to(a[:, j:j + 1], a.shape)


def _ssd_kernel(xp_ref, xc_ref, xn_ref, bp_ref, bc_ref, bn_ref, dt_ref, z_ref, part_ref, wx_ref, bx_ref, wb_ref,
                bb_ref, dtb_ref, aneg_ref, skip_ref, nw_ref, o_ref, h_ref, *, rev):
    c = pl.program_id(1)
    nchunk = pl.num_programs(1)
    first, last = c == 0, c == nchunk - 1
    has_prev = jnp.logical_not(last) if rev else jnp.logical_not(first)
    has_next = jnp.logical_not(first) if rev else jnp.logical_not(last)

    @pl.when(first)
    def _():
        h_ref[...] = jnp.zeros_like(h_ref)

    xs = _conv_silu(xp_ref, xc_ref, xn_ref, wx_ref, bx_ref, has_prev, has_next)
    bc = _conv_silu(bp_ref, bc_ref, bn_ref, wb_ref, bb_ref, has_prev, has_next)
    gn = SSM_GROUPS * SSM_STATE
    bm, cm = bc[:, :gn], bc[:, gn:]

    dt_raw = dt_ref[...]
    if rev:
        dt_raw = pltpu.roll(dt_raw, DT_LANES - SSM_HEADS, axis=1)
    xb = dt_raw + dtb_ref[...]
    dt = jnp.maximum(xb, 0.0) + jnp.log(1.0 + jnp.exp(-jnp.abs(xb)))
    da = dt * aneg_ref[...]

    row = lax.broadcasted_iota(jnp.int32, (SSM_CHUNK, SSM_CHUNK), 0)
    col = lax.broadcasted_iota(jnp.int32, (SSM_CHUNK, SSM_CHUNK), 1)
    keep = (col >= row) if rev else (col <= row)
    cs = jnp.dot(keep.astype(jnp.float32), da, precision=lax.Precision.HIGHEST,
                 preferred_element_type=jnp.float32)
    cs_t = cs.T
    end_row = 0 if rev else SSM_CHUNK - 1

    lane = lax.broadcasted_iota(jnp.int32, (SSM_CHUNK, LANES), 1)
    low = lane < SSM_HEAD_DIM
    nt = (((1,), (1,)), ((), ()))
    bf = jnp.bfloat16
    heads_per_group = SSM_HEADS // SSM_GROUPS
    y_chunks = []
    for pair in range(SSM_HEADS // 2):
        g = (2 * pair) // heads_per_group
        bm_g = bm[:, g * SSM_STATE:(g + 1) * SSM_STATE]
        cm_g = cm[:, g * SSM_STATE:(g + 1) * SSM_STATE]
        if pair % (heads_per_group // 2) == 0:
            cb = lax.dot_general(cm_g.astype(bf), bm_g.astype(bf), nt, preferred_element_type=jnp.float32)
            bm_t = bm_g.T.astype(bf)
        xs_c = xs[:, pair * LANES:(pair + 1) * LANES]
        per_head = []
        for j in (2 * pair, 2 * pair + 1):
            ccol = _lane_col(cs, j)
            dec = jnp.where(keep, jnp.exp(jnp.where(keep, ccol - cs_t[j:j + 1, :], 0.0)), 0.0)
            total = ccol[end_row:end_row + 1, :]
            per_head.append(((cb * dec).astype(bf), _lane_col(dt, j), jnp.exp(ccol), jnp.exp(total - ccol),
                             jnp.exp(total)))
        (m_a, d_a, e_a, t_a, g_a), (m_b, d_b, e_b, t_b, g_b) = per_head
        xdt = xs_c * jnp.where(low, d_a, d_b)
        xdt_b = xdt.astype(bf)
        y_diag = jnp.where(low, jnp.dot(m_a, xdt_b, preferred_element_type=jnp.float32),
                           jnp.dot(m_b, xdt_b, preferred_element_type=jnp.float32))
        h = h_ref[pair]
        y_off = jnp.dot(cm_g.astype(bf), h.astype(bf), preferred_element_type=jnp.float32) * jnp.where(low, e_a, e_b)
        states = jnp.dot(bm_t, (xdt * jnp.where(low, t_a, t_b)).astype(bf), preferred_element_type=jnp.float32)
        h_ref[pair] = h * jnp.where(low, g_a, g_b) + states
        y = y_diag + y_off
        if not rev:
            y = y + xs_c * skip_ref[:, pair * LANES:(pair + 1) * LANES]
        y_chunks.append(y)
    y = jnp.concatenate(y_chunks, axis=1)
    if not rev:
        o_ref[...] = y
    else:
        z = z_ref[...]
        y = (y + part_ref[...]) * (z * jax.nn.sigmoid(z))
        o_ref[...] = y * lax.rsqrt(jnp.mean(y * y, axis=-1, keepdims=True) + EPS) * nw_ref[...]


def _ssd_mixer(proj, p, batch, seq):
    nchunk = seq // SSM_CHUNK
    assert seq % SSM_CHUNK == 0
    f32 = jnp.float32
    halo_per_chunk = SSM_CHUNK // HALO
    xcol = 1
    bc_w = 2 * SSM_GROUPS * SSM_STATE
    bc_col = (2 * SSM_INNER) // bc_w
    dt_col = (2 * SSM_INNER + bc_w) // DT_LANES
    pad16 = lambda a: jnp.pad(a.astype(f32), (0, LANES - SSM_HEADS)).reshape(1, LANES)
    per_lane = lambda a: jnp.repeat(a.astype(f32), SSM_HEAD_DIM).reshape(1, SSM_INNER)
    wx, wb = p['conv_w'][:, :SSM_INNER].astype(f32), p['conv_w'][:, SSM_INNER:].astype(f32)
    bx, bb = p['conv_b'][:SSM_INNER].reshape(1, -1).astype(f32), p['conv_b'][SSM_INNER:].reshape(1, -1).astype(f32)
    skip = per_lane(p['d_skip'])
    nw = p['ssm_norm_w'].reshape(1, SSM_INNER).astype(f32)

    def run(rev, part):
        pos = (lambda c: nchunk - 1 - c) if rev else (lambda c: c)

        def chunk(w, colblk):
            return pl.BlockSpec((SSM_CHUNK, w), lambda b, c: (b * nchunk + pos(c), colblk))

        def halo(w, colblk, side):
            def index(b, c):
                r = (b * nchunk + pos(c)) * halo_per_chunk
                r = r - 1 if side < 0 else r + halo_per_chunk
                return (jnp.clip(r, 0, batch * nchunk * halo_per_chunk - 1), colblk)
            return pl.BlockSpec((HALO, w), index)

        const = lambda a: pl.BlockSpec(a.shape, lambda b, c: (0, 0))
        dtb = pad16(p['dt_bias_b'] if rev else p['dt_bias_f'])
        aneg = pad16(-jnp.exp((p['a_log_b'] if rev else p['a_log_f']).astype(f32)))
        consts = (wx, bx, wb, bb, dtb, aneg, skip, nw)
        return pl.pallas_call(
            functools.partial(_ssd_kernel, rev=rev),
            grid=(batch, nchunk),
            in_specs=[halo(SSM_INNER, xcol, -1), chunk(SSM_INNER, xcol), halo(SSM_INNER, xcol, 1),
                      halo(bc_w, bc_col, -1), chunk(bc_w, bc_col), halo(bc_w, bc_col, 1),
                      chunk(DT_LANES, dt_col), chunk(SSM_INNER, 0), chunk(SSM_INNER, 0)] + [const(a) for a in consts],
            out_specs=chunk(SSM_INNER, 0),
            out_shape=jax.ShapeDtypeStruct((batch * seq, SSM_INNER), f32),
            scratch_shapes=[pltpu.VMEM((SSM_HEADS // 2, SSM_STATE, LANES), f32)],
            compiler_params=pltpu.CompilerParams(dimension_semantics=("parallel", "arbitrary"),
                                                 vmem_limit_bytes=VMEM_LIMIT_BYTES),
            name="ssd_bwd" if rev else "ssd_fwd",
        )(proj, proj, proj, proj, proj, proj, proj, proj, part, *consts)

    return run(True, run(False, proj))


ROUTE_TM = 256
CAND_GROUPS = ((0, 16), (1, 8), (2, 8), (3, 8), (4, 8), (5, 8), (6, 8), (7, 8))
CAND_ROWS = sum(n for _, n in CAND_GROUPS) + SUBLANES
NO_PAIR = PEER_TOPK * PEER_TOPK
TOPK_SHIFT = PEER_TOPK.bit_length() - 1
assert 1 << TOPK_SHIFT == PEER_TOPK


def _cand_flat_ids():
    flat = []
    for a, n in CAND_GROUPS:
        flat += [a * PEER_TOPK + b if (a + 1) * (b + 1) <= PEER_TOPK else NO_PAIR for b in range(n)]
    flat += [a * PEER_TOPK for a in range(SUBLANES, PEER_TOPK)]
    return jnp.broadcast_to(jnp.array(flat, jnp.int32)[:, None], (CAND_ROWS, LANES))


def _top_rows(s, row_ids, k, sentinel):
    slot = lax.broadcasted_iota(jnp.int32, (k, LANES), 0)
    vals = jnp.zeros((k, LANES), jnp.float32)
    ids = jnp.zeros((k, LANES), jnp.int32)
    for r in range(k):
        m = jnp.max(s, axis=0, keepdims=True)
        i = jnp.min(jnp.where(s == m, row_ids, sentinel), axis=0, keepdims=True)
        vals = jnp.where(slot == r, m, vals)
        ids = jnp.where(slot == r, i, ids)
        s = jnp.where(row_ids == i, -jnp.inf, s)
    return vals, ids


def _lookup_rows(table, pos):
    slot = lax.broadcasted_iota(jnp.int32, table.shape, 0)
    return jnp.sum(jnp.where(slot == pos, table, 0), axis=0, keepdims=True)


def _peer_route_kernel(x_ref, g_ref, wq_ref, k1_ref, k2_ref, flat_ref, xn_ref, idx_ref, gate_ref, xb_ref):
    nt = (((1,), (1,)), ((), ()))

    @pl.when(pl.program_id(1) == 0)
    def _():
        x = x_ref[...]
        xn = x * lax.rsqrt(jnp.mean(x * x, axis=-1, keepdims=True) + EPS) * g_ref[...]
        xn_ref[...] = xn
        xb_ref[...] = xn.astype(jnp.bfloat16)

    q = jnp.dot(xb_ref[...], wq_ref[...], preferred_element_type=jnp.float32).astype(jnp.bfloat16)
    half = PEER_QDIM // 2
    s1 = lax.dot_general(k1_ref[...], q[:, :half], nt, preferred_element_type=jnp.float32)
    s2 = lax.dot_general(k2_ref[...], q[:, half:], nt, preferred_element_type=jnp.float32)
    key_ids = lax.broadcasted_iota(jnp.int32, (PEER_N_KEYS, LANES), 0)
    flat = flat_ref[...]
    for lt in range(ROUTE_TM // LANES):
        lanes = slice(lt * LANES, (lt + 1) * LANES)
        v1, i1 = _top_rows(s1[:, lanes], key_ids, PEER_TOPK, PEER_N_KEYS)
        v2, i2 = _top_rows(s2[:, lanes], key_ids, PEER_TOPK, PEER_N_KEYS)
        blocks = [v1[a:a + 1] + v2[:n] for a, n in CAND_GROUPS] + [v1[SUBLANES:] + v2[0:1]]
        cand = jnp.where(flat < NO_PAIR, jnp.concatenate(blocks, axis=0), -jnp.inf)
        top_s, top_flat = _top_rows(cand, flat, PEER_TOPK, 2 * NO_PAIR)
        ids = jnp.concatenate(
            [_lookup_rows(i1, top_flat[r:r + 1] >> TOPK_SHIFT) * PEER_N_KEYS + _lookup_rows(i2, top_flat[r:r + 1] & (PEER_TOPK - 1))
             for r in range(PEER_TOPK)], axis=0)
        e = jnp.exp(top_s - top_s[0:1])
        idx_ref[:, lanes] = ids * PACK_ROWS
        gate_ref[:, lanes] = e / jnp.sum(e, axis=0, keepdims=True)


def _peer_route(x, gain, wq, k1, k2):
    t = x.shape[0]
    assert t % ROUTE_TM == 0
    xn, idx_t, gate_t = pl.pallas_call(
        _peer_route_kernel,
        grid=(t // ROUTE_TM, PEER_HEADS),
        in_specs=[pl.BlockSpec((ROUTE_TM, D_MODEL), lambda i, h: (i, 0)),
                  pl.BlockSpec((1, D_MODEL), lambda i, h: (0, 0)),
                  pl.BlockSpec((D_MODEL, PEER_QDIM), lambda i, h: (0, h)),
                  pl.BlockSpec((PEER_N_KEYS, PEER_QDIM // 2), lambda i, h: (0, 0)),
                  pl.BlockSpec((PEER_N_KEYS, PEER_QDIM // 2), lambda i, h: (0, 0)),
                  pl.BlockSpec((CAND_ROWS, LANES), lambda i, h: (0, 0))],
        out_specs=[pl.BlockSpec((ROUTE_TM, D_MODEL), lambda i, h: (i, 0)),
                   pl.BlockSpec((PEER_TOPK, ROUTE_TM), lambda i, h: (h, i)),
                   pl.BlockSpec((PEER_TOPK, ROUTE_TM), lambda i, h: (h, i))],
        out_shape=[jax.ShapeDtypeStruct((t, D_MODEL), jnp.float32),
                   jax.ShapeDtypeStruct((PEER_PICKS, t), jnp.int32),
                   jax.ShapeDtypeStruct((PEER_PICKS, t), jnp.float32)],
        scratch_shapes=[pltpu.VMEM((ROUTE_TM, D_MODEL), jnp.bfloat16)],
        compiler_params=pltpu.CompilerParams(dimension_semantics=("parallel", "arbitrary"),
                                             vmem_limit_bytes=VMEM_LIMIT_BYTES),
        name="peer_route",
    )(x, gain.reshape(1, D_MODEL), wq.astype(jnp.bfloat16), k1.astype(jnp.bfloat16), k2.astype(jnp.bfloat16),
      _cand_flat_ids())
    return xn, idx_t.T, gate_t.T


def _encoder_layer(x, p):
    b, l, _ = x.shape
    t = b * l
    xt = x.reshape(t, D_MODEL)
    w_in, gain = p['w_in'], p['norm1_w']
    ssm_w, swa_w, dil_w = sum(IN_SPLITS[:6]), sum(IN_SPLITS[6:9]), sum(IN_SPLITS[9:12])
    ssm_proj = _matmul(xt, w_in[:, :ssm_w], gain=gain, tn=384, keep_pad=True)
    swa_proj = _matmul(xt, w_in[:, ssm_w:ssm_w + swa_w], gain=gain, tn=768)
    dil_proj = _matmul(xt, w_in[:, ssm_w + swa_w:ssm_w + swa_w + dil_w], gain=gain, tn=768)
    gate_logits = _matmul(xt, w_in[:, ssm_w + swa_w + dil_w:], gain=gain, tn=1024)
    y_a = _ssd_mixer(ssm_proj, p, b, l)
    cos, sin = _rope_tables(l)
    y_b = _window_attention(swa_proj, cos, sin, p['q_norm_b'], p['k_norm_b'], p['sink_b'], b, l)
    y_c = _dilated_attention(dil_proj, cos, sin, p['q_norm_c'], p['k_norm_c'], b, l)
    xt = _mixer_out(xt, y_a, y_b, y_c, gate_logits, p['w_a'], p['w_b'], p['w_c'], p['w_out'])
    xn, idx, gate = _peer_route(xt, p['norm2_w'], p['peer_wq'], p['peer_k1'], p['peer_k2'])
    xt = _peer_experts(xt, xn, idx, gate, p['u_packed'], p['v_packed'])
    return xt.reshape(b, l, D_MODEL)


_LAYER_KEYS = ('norm1_w', 'w_in', 'conv_w', 'conv_b', 'dt_bias_f', 'dt_bias_b', 'a_log_f', 'a_log_b', 'd_skip',
               'ssm_norm_w', 'w_a', 'q_norm_b', 'k_norm_b', 'sink_b', 'w_b', 'q_norm_c', 'k_norm_c', 'w_c', 'w_out',
               'norm2_w', 'peer_wq', 'peer_k1', 'peer_k2', 'peer_u', 'peer_v')


def kernel(x_prompt, x_sample, norm1_w, w_in, conv_w, conv_b, dt_bias_f, dt_bias_b, a_log_f, a_log_b, d_skip, ssm_norm_w, w_a, q_norm_b, k_norm_b, sink_b, w_b, q_norm_c, k_norm_c, w_c, w_out, norm2_w, peer_wq, peer_k1, peer_k2, peer_u, peer_v):
    stacked = dict(zip(_LAYER_KEYS, (norm1_w, w_in, conv_w, conv_b, dt_bias_f, dt_bias_b, a_log_f, a_log_b, d_skip,
                                     ssm_norm_w, w_a, q_norm_b, k_norm_b, sink_b, w_b, q_norm_c, k_norm_c, w_c, w_out,
                                     norm2_w, peer_wq, peer_k1, peer_k2, peer_u, peer_v)))
    layers = []
    for i in range(w_in.shape[0]):
        p = {k: v[i] for k, v in stacked.items()}
        p['u_packed'] = _pack_table(p['peer_u'])
        p['v_packed'] = _pack_table(p['peer_v'])
        layers.append(p)

    def trunk(x):
        for p in layers:
            x = _encoder_layer(x, p)
        return x

    return trunk(x_prompt), trunk(x_sample)
```

```python
import functools
import math

import jax
import jax.numpy as jnp
from jax import lax
from jax.experimental import pallas as pl
from jax.experimental.pallas import tpu as pltpu

D_MODEL = 1024
HEAD_DIM = 64
SSM_HEADS = 16
SSM_HEAD_DIM = 64
SSM_INNER = SSM_HEADS * SSM_HEAD_DIM
SSM_GROUPS = 2
SSM_STATE = 128
SSM_CHUNK = 128
CONV_WIDTH = 5
CONV_CH = SSM_INNER + 2 * SSM_GROUPS * SSM_STATE
SWA_HEADS = 16
SWA_KV_HEADS = 4
SWA_WINDOW = 128
SWA_BLOCK = 128
DIL_PATTERNS = ((128, 1), (512, 4), (2048, 16))
DIL_HEADS_PER_GROUP = 4
DIL_HEADS = DIL_HEADS_PER_GROUP * 3
DIL_OUT = DIL_HEADS_PER_GROUP * HEAD_DIM
PEER_HEADS = 8
PEER_N_KEYS = 128
PEER_EXPERTS = PEER_N_KEYS * PEER_N_KEYS
PEER_TOPK = 16
PEER_QDIM = 256
PEER_PICKS = PEER_HEADS * PEER_TOPK
N_BRANCHES = 3
ROPE_THETA = 10000.0
EPS = 1e-6
NEG_INF = -1e30
IN_SPLITS = (SSM_INNER, SSM_INNER, SSM_GROUPS * SSM_STATE, SSM_GROUPS * SSM_STATE, SSM_HEADS, SSM_HEADS,
             SWA_HEADS * HEAD_DIM, SWA_KV_HEADS * HEAD_DIM, SWA_KV_HEADS * HEAD_DIM,
             DIL_HEADS * HEAD_DIM, DIL_HEADS * HEAD_DIM, DIL_HEADS * HEAD_DIM,
             N_BRANCHES * D_MODEL)
IN_COLS = sum(IN_SPLITS)

LANES = 128
SUBLANES = 8
VMEM_LIMIT_BYTES = 56 * 1024 * 1024

PACK_WORDS = D_MODEL // 2
PACK_ROWS = PACK_WORDS // LANES
ROW_SUBLANES = D_MODEL // LANES
PEER_TB = 32


def _mm_kernel(x_ref, g_ref, w_ref, o_ref, xn_ref, *, norm):
    @pl.when(pl.program_id(1) == 0)
    def _():
        x = x_ref[...].astype(jnp.float32)
        if norm:
            x = x * lax.rsqrt(jnp.mean(x * x, axis=-1, keepdims=True) + EPS) * g_ref[...]
        xn_ref[...] = x.astype(jnp.bfloat16)

    o_ref[...] = jnp.dot(xn_ref[...], w_ref[...], preferred_element_type=jnp.float32)


def _matmul(x, w, gain=None, tm=1024, tn=512, keep_pad=False):
    m, k = x.shape
    n = w.shape[1]
    tm = min(tm, m)
    tn = min(tn, n)
    assert m % tm == 0
    n_pad = -n % tn
    wb = w.astype(jnp.bfloat16)
    if n_pad:
        wb = jnp.pad(wb, ((0, 0), (0, n_pad)))
    g = jnp.ones((1, k), jnp.float32) if gain is None else gain.reshape(1, k).astype(jnp.float32)
    out = pl.pallas_call(
        functools.partial(_mm_kernel, norm=gain is not None),
        grid=(m // tm, (n + n_pad) // tn),
        in_specs=[pl.BlockSpec((tm, k), lambda i, j: (i, 0)),
                  pl.BlockSpec((1, k), lambda i, j: (0, 0)),
                  pl.BlockSpec((k, tn), lambda i, j: (0, j))],
        out_specs=pl.BlockSpec((tm, tn), lambda i, j: (i, j)),
        out_shape=jax.ShapeDtypeStruct((m, n + n_pad), jnp.float32),
        scratch_shapes=[pltpu.VMEM((tm, k), jnp.bfloat16)],
        compiler_params=pltpu.CompilerParams(dimension_semantics=("parallel", "arbitrary"),
                                             vmem_limit_bytes=VMEM_LIMIT_BYTES),
        name="proj_matmul",
    )(x, g, wb)
    return out[:, :n] if n_pad and not keep_pad else out


def _mixer_out_kernel(x_ref, ya_ref, yb_ref, yc_ref, gl_ref, wa_ref, wb_ref, wc_ref, wo_ref, o_ref):
    def proj(y_ref, w_ref):
        return jnp.dot(y_ref[...].astype(jnp.bfloat16), w_ref[...], preferred_element_type=jnp.float32)

    gates = jax.nn.sigmoid(gl_ref[...])
    merged = (gates[:, :D_MODEL] * proj(ya_ref, wa_ref) + gates[:, D_MODEL:2 * D_MODEL] * proj(yb_ref, wb_ref)
              + gates[:, 2 * D_MODEL:] * proj(yc_ref, wc_ref))
    o_ref[...] = x_ref[...] + jnp.dot(merged.astype(jnp.bfloat16), wo_ref[...], preferred_element_type=jnp.float32)


def _mixer_out(x, ya, yb, yc, gate_logits, w_a, w_b, w_c, w_out, tm=512):
    t = x.shape[0]
    assert t % tm == 0
    ws = [w.astype(jnp.bfloat16) for w in (w_a, w_b, w_c, w_out)]
    row = lambda a: pl.BlockSpec((tm, a.shape[1]), lambda i: (i, 0))
    full = lambda a: pl.BlockSpec(a.shape, lambda i: (0, 0))
    acts = (x, ya, yb, yc, gate_logits)
    return pl.pallas_call(
        _mixer_out_kernel,
        grid=(t // tm,),
        in_specs=[row(a) for a in acts] + [full(w) for w in ws],
        out_specs=row(x),
        out_shape=jax.ShapeDtypeStruct(x.shape, jnp.float32),
        compiler_params=pltpu.CompilerParams(dimension_semantics=("parallel",), vmem_limit_bytes=VMEM_LIMIT_BYTES),
        name="mixer_out",
    )(*acts, *ws)


def _pack_table(tab):
    e = tab.shape[0]
    tb = lax.bitcast_convert_type(tab.astype(jnp.bfloat16), jnp.uint16).astype(jnp.uint32)
    tb = tb.reshape(e, PACK_ROWS, 2, LANES)
    packed = tb[:, :, 0, :] | (tb[:, :, 1, :] << 16)
    return lax.bitcast_convert_type(packed, jnp.int32).reshape(e * PACK_ROWS, LANES)


def _peer_constants():
    col = jnp.arange(PEER_PICKS * ROW_SUBLANES)
    diag = (col[None, :] % ROW_SUBLANES == jnp.arange(ROW_SUBLANES)[:, None]).astype(jnp.float32)
    group = (col[:, None] // ROW_SUBLANES == jnp.arange(PEER_PICKS)[None, :]).astype(jnp.bfloat16)
    return diag, group, group.T


def _split2(x):
    hi = x.astype(jnp.bfloat16).astype(jnp.float32)
    return hi, x - hi


def _stack_pieces(pieces):
    return jnp.concatenate(list(pieces), axis=1).astype(jnp.bfloat16)


def _sum_pieces(y):
    return y[:ROW_SUBLANES] + y[ROW_SUBLANES:]


GATHER_BATCH = 16


def _gather_rows(idx_ref, tab_ref, t, raw_ref):
    for j0 in range(0, PEER_PICKS, GATHER_BATCH):
        rows = [tab_ref[pl.ds(pl.multiple_of(idx_ref[t, j], PACK_ROWS), PACK_ROWS), :]
                for j in range(j0, j0 + GATHER_BATCH)]
        for k in range(0, GATHER_BATCH, 2):
            raw_ref[pl.ds((j0 + k) * PACK_ROWS, 2 * PACK_ROWS), :] = jnp.concatenate(rows[k:k + 2], axis=0)


def _token_rows(t):
    return pl.ds(t * ROW_SUBLANES, ROW_SUBLANES)


def _pipelined_tokens(gather, compute, bufs):
    gather(0, bufs[0])
    for t in range(PEER_TB):
        compute(t, bufs[t % 2])
        if t + 1 < PEER_TB:
            gather(t + 1, bufs[(t + 1) % 2])


def _with_staged_ids(idx_hbm, idx_smem, sem, body):
    i = pl.program_id(0)
    n = pl.num_programs(0)

    def copy(step, slot):
        return pltpu.make_async_copy(idx_hbm.at[pl.ds(step * PEER_TB, PEER_TB)], idx_smem.at[slot], sem.at[slot])

    @pl.when(i == 0)
    def _():
        copy(0, 0).start()

    for slot in range(2):
        @pl.when(i % 2 == slot)
        def _():
            @pl.when(i + 1 < n)
            def _():
                copy(i + 1, 1 - slot).start()

            copy(i, slot).wait()
            body(idx_smem.at[slot])


def _peer_act_kernel(idx_hbm, x_ref, gate_ref, diag_ref, group_ref, tab_ref, w_ref, raw0_ref, raw1_ref, x3_ref,
                     picked_ref, idx_smem, sem):
    nt = (((1,), (1,)), ((), ()))
    x3_ref[...] = _stack_pieces(_split2(x_ref[...].reshape(PEER_TB, ROW_SUBLANES, LANES)))

    def compute(t, buf_ref):
        wmat = pltpu.bitcast(buf_ref[...], jnp.bfloat16)
        scores = _sum_pieces(lax.dot_general(x3_ref[t], wmat, nt, preferred_element_type=jnp.float32))
        picked_ref[_token_rows(t), :] = scores * diag_ref[...]

    _with_staged_ids(idx_hbm, idx_smem, sem, lambda ids: _pipelined_tokens(
        functools.partial(_gather_rows, ids, tab_ref), compute, (raw0_ref, raw1_ref)))

    act8 = sum(jnp.dot(piece.astype(jnp.bfloat16), group_ref[...], preferred_element_type=jnp.float32)
               for piece in _split2(picked_ref[...]))
    act = jnp.sum(act8.reshape(PEER_TB, ROW_SUBLANES, PEER_PICKS), axis=1)
    gelu = 0.5 * act * (1.0 + lax.erf(act * (2.0 ** -0.5)))
    w_ref[...] = gate_ref[...] * gelu


def _peer_out_kernel(idx_hbm, w_ref, res_ref, diag_ref, expand_ref, tab_ref, o_ref, raw0_ref, raw1_ref, lhs_ref,
                     idx_smem, sem):
    wide = [jnp.dot(piece.astype(jnp.bfloat16), expand_ref[...], preferred_element_type=jnp.float32)
            for piece in _split2(w_ref[...])]
    diag = diag_ref[...]
    for t in range(PEER_TB):
        rows = [jnp.broadcast_to(wd[t:t + 1], diag.shape) * diag for wd in wide]
        lhs_ref[t] = jnp.concatenate(rows, axis=0).astype(jnp.bfloat16)

    def compute(t, buf_ref):
        wmat = pltpu.bitcast(buf_ref[...], jnp.bfloat16)
        rows = _token_rows(t)
        o_ref[rows, :] = res_ref[rows, :] + _sum_pieces(jnp.dot(lhs_ref[t], wmat, preferred_element_type=jnp.float32))

    _with_staged_ids(idx_hbm, idx_smem, sem, lambda ids: _pipelined_tokens(
        functools.partial(_gather_rows, ids, tab_ref), compute, (raw0_ref, raw1_ref)))


def _const_spec(shape):
    return pl.BlockSpec(shape, lambda i: (0,) * len(shape), pipeline_mode=pl.Buffered(1))


def _peer_experts(x_res, xn, idx, gate, u_packed, v_packed):
    t = xn.shape[0]
    assert t % PEER_TB == 0 and PEER_TB % 2 == 0
    grid = (t // PEER_TB,)
    params = pltpu.CompilerParams(dimension_semantics=("arbitrary",), vmem_limit_bytes=VMEM_LIMIT_BYTES)
    ids_spec = pl.BlockSpec(memory_space=pl.ANY)
    ids_scratch = [pltpu.SMEM((2, PEER_TB, PEER_PICKS), jnp.int32), pltpu.SemaphoreType.DMA((2,))]
    rows_spec = pl.BlockSpec((PEER_TB * ROW_SUBLANES, LANES), lambda i: (i, 0))
    picks_spec = pl.BlockSpec((PEER_TB, PEER_PICKS), lambda i: (i, 0))
    raw = pltpu.VMEM((PEER_PICKS * PACK_ROWS, LANES), jnp.int32)
    diag, group, expand = _peer_constants()
    stacked = 2 * ROW_SUBLANES

    w = pl.pallas_call(
        _peer_act_kernel,
        grid=grid,
        in_specs=[ids_spec, rows_spec, picks_spec, _const_spec(diag.shape), _const_spec(group.shape),
                  _const_spec(u_packed.shape)],
        out_specs=picks_spec,
        out_shape=jax.ShapeDtypeStruct((t, PEER_PICKS), jnp.float32),
        scratch_shapes=[raw, raw, pltpu.VMEM((PEER_TB, stacked, LANES), jnp.bfloat16),
                        pltpu.VMEM((PEER_TB * ROW_SUBLANES, PEER_PICKS * ROW_SUBLANES), jnp.float32)] + ids_scratch,
        compiler_params=params,
        name="peer_act",
    )(idx, xn.reshape(t * ROW_SUBLANES, LANES), gate, diag, group, u_packed)

    out = pl.pallas_call(
        _peer_out_kernel,
        grid=grid,
        in_specs=[ids_spec, picks_spec, rows_spec, _const_spec(diag.shape), _const_spec(expand.shape),
                  _const_spec(v_packed.shape)],
        out_specs=rows_spec,
        out_shape=jax.ShapeDtypeStruct((t * ROW_SUBLANES, LANES), jnp.float32),
        scratch_shapes=[raw, raw, pltpu.VMEM((PEER_TB, stacked, PEER_PICKS * ROW_SUBLANES), jnp.bfloat16)]
        + ids_scratch,
        compiler_params=params,
        name="peer_out",
    )(idx, w, x_res.reshape(t * ROW_SUBLANES, LANES), diag, expand, v_packed)
    return out.reshape(t, D_MODEL)


ATT_BLK = 128
HALF_HEAD = HEAD_DIM // 2


def _rope_tables(seq):
    inv_freq = ROPE_THETA ** (-jnp.arange(HALF_HEAD, dtype=jnp.float32) / HALF_HEAD)
    ang = jnp.arange(seq).astype(jnp.float32)[:, None] * inv_freq[None, :]
    cos, sin = jnp.cos(ang), jnp.sin(ang)
    return jnp.tile(jnp.concatenate([cos, cos], axis=1), (1, 2)), jnp.tile(jnp.concatenate([-sin, sin], axis=1), (1, 2))


def _norm_rope(a, gain, cos, sin):
    lane = lax.broadcasted_iota(jnp.int32, (a.shape[0], LANES), 1)
    low_head = lane < HEAD_DIM
    first_half = (lane % HEAD_DIM) < HALF_HEAD
    out = []
    for c in range(a.shape[1] // LANES):
        x = a[:, c * LANES:(c + 1) * LANES]
        sq = x * x
        lo = jnp.sum(jnp.where(low_head, sq, 0.0), axis=1, keepdims=True)
        hi = jnp.sum(jnp.where(low_head, 0.0, sq), axis=1, keepdims=True)
        y = x * lax.rsqrt(jnp.where(low_head, lo, hi) * (1.0 / HEAD_DIM) + EPS) * gain
        rot = jnp.where(first_half, pltpu.roll(y, LANES - HALF_HEAD, axis=1), pltpu.roll(y, HALF_HEAD, axis=1))
        out.append(y * cos + rot * sin)
    return out


def _attn_kernel(sink_ref, q_ref, kp_ref, kc_ref, kn_ref, vp_ref, vc_ref, vn_ref, cp_ref, cc_ref, cn_ref,
                 sp_ref, sc_ref, sn_ref, qg_ref, kg_ref, o_ref, *lse_refs, hq, hkv, window, use_sink):
    n = pl.program_id(2)
    nblk = pl.num_programs(2)
    nt = (((1,), (1,)), ((), ()))
    scale = HEAD_DIM ** -0.5
    q_chunks = _norm_rope(q_ref[...], qg_ref[...], cc_ref[...], sc_ref[...])
    k_parts = [_norm_rope(k_ref[...], kg_ref[...], c_ref[...], s_ref[...])
               for k_ref, c_ref, s_ref in ((kp_ref, cp_ref, sp_ref), (kc_ref, cc_ref, sc_ref), (kn_ref, cn_ref, sn_ref))]
    k_chunks = [jnp.concatenate([part[c] for part in k_parts], axis=0).astype(jnp.bfloat16)
                for c in range(hkv // 2)]
    v_all = jnp.concatenate([vp_ref[...], vc_ref[...], vn_ref[...]], axis=0).astype(jnp.bfloat16)

    side = kp_ref.shape[0]
    keys = ATT_BLK + 2 * side
    row = lax.broadcasted_iota(jnp.int32, (ATT_BLK, keys), 0)
    col = lax.broadcasted_iota(jnp.int32, (ATT_BLK, keys), 1)
    rel = col - side - row
    first_col = jnp.where(n > 0, 0, side)
    end_col = jnp.where(n < nblk - 1, keys, side + ATT_BLK)
    mask = (jnp.abs(rel) <= window) & (col >= first_col) & (col < end_col)
    lane = lax.broadcasted_iota(jnp.int32, (ATT_BLK, LANES), 1)
    halves = (lane < HEAD_DIM, lane >= HEAD_DIM)

    group = hq // hkv
    outs = [jnp.zeros((ATT_BLK, LANES), jnp.float32) for _ in range(hq // 2)]
    lses = [jnp.zeros((ATT_BLK, LANES), jnp.float32) for _ in range(hq // 2)]
    for h in range(hq):
        g = h // group
        qh = q_chunks[h // 2]
        if h % 2 != g % 2:
            qh = pltpu.roll(qh, HEAD_DIM, axis=1)
        qh = (jnp.where(halves[g % 2], qh, 0.0)).astype(jnp.bfloat16)
        s = lax.dot_general(qh, k_chunks[g // 2], nt, preferred_element_type=jnp.float32) * scale
        s = jnp.where(mask, s, NEG_INF)
        m = jnp.max(s, axis=1, keepdims=True)
        if use_sink:
            m = jnp.maximum(m, sink_ref[h])
        p = jnp.exp(s - m)
        den = jnp.sum(p, axis=1, keepdims=True)
        if use_sink:
            den = den + jnp.exp(sink_ref[h] - m)
        vg = v_all[:, (g // 2) * LANES:(g // 2 + 1) * LANES]
        o = jnp.dot(p.astype(jnp.bfloat16), vg, preferred_element_type=jnp.float32) / den
        if h % 2 != g % 2:
            o = pltpu.roll(o, HEAD_DIM, axis=1)
        outs[h // 2] = jnp.where(halves[h % 2], o, outs[h // 2])
        if lse_refs:
            lses[h // 2] = jnp.where(halves[h % 2], m + jnp.log(den), lses[h // 2])
    o_ref[...] = jnp.concatenate(outs, axis=1)
    if lse_refs:
        lse_refs[0][...] = jnp.concatenate(lses, axis=1)


def _banded_attention(src, cos, sin, q_gain, k_gain, sink, *, batch, seq, dil, hq, hkv, q_off, k_off, v_off, window,
                      want_lse):
    rows = seq // dil
    assert rows % ATT_BLK == 0 and src.shape[0] == batch * rows and src.shape[1] % dil == 0
    nblk = rows // ATT_BLK
    width = src.shape[1] // dil
    qw, kw = hq * HEAD_DIM, hkv * HEAD_DIM
    assert all((r * width + off) % w == 0 for r in range(dil) for off, w in ((q_off, qw), (k_off, kw), (v_off, kw)))

    side = -(-window // SUBLANES) * SUBLANES
    assert side <= ATT_BLK and ATT_BLK % side == 0
    per_blk = ATT_BLK // side
    nside = rows // side

    def neighbour(n, shift):
        return jnp.clip(n * per_blk - 1 if shift < 0 else (n + 1) * per_blk, 0, nside - 1)

    def spec(w, off, shift):
        col = lambda r: (r * width + off) // w
        if shift == 0:
            return pl.BlockSpec((ATT_BLK, w), lambda b, r, n: (b * nblk + n, col(r)))
        return pl.BlockSpec((side, w), lambda b, r, n: (b * nside + neighbour(n, shift), col(r)))

    def table_spec(shift):
        if shift == 0:
            return pl.BlockSpec((ATT_BLK, LANES), lambda b, r, n: (n, r))
        return pl.BlockSpec((side, LANES), lambda b, r, n: (neighbour(n, shift), r))

    tile2 = lambda gain: jnp.tile(gain.astype(jnp.float32), 2).reshape(1, LANES)
    gain_spec = pl.BlockSpec((1, LANES), lambda b, r, n: (0, 0))
    out_spec = pl.BlockSpec((ATT_BLK, qw), lambda b, r, n: (b * nblk + n, r))
    out_shape = jax.ShapeDtypeStruct((batch * rows, dil * qw), jnp.float32)
    use_sink = sink is not None
    sink_arr = sink.astype(jnp.float32) if use_sink else jnp.zeros((hq,), jnp.float32)
    cos_d, sin_d = cos.reshape(rows, dil * LANES), sin.reshape(rows, dil * LANES)
    res = pl.pallas_call(
        functools.partial(_attn_kernel, hq=hq, hkv=hkv, window=window, use_sink=use_sink),
        grid=(batch, dil, nblk),
        in_specs=[pl.BlockSpec(memory_space=pltpu.SMEM), spec(qw, q_off, 0),
                  spec(kw, k_off, -1), spec(kw, k_off, 0), spec(kw, k_off, 1),
                  spec(kw, v_off, -1), spec(kw, v_off, 0), spec(kw, v_off, 1),
                  table_spec(-1), table_spec(0), table_spec(1), table_spec(-1), table_spec(0), table_spec(1),
                  gain_spec, gain_spec],
        out_specs=[out_spec, out_spec] if want_lse else [out_spec],
        out_shape=[out_shape, out_shape] if want_lse else [out_shape],
        compiler_params=pltpu.CompilerParams(dimension_semantics=("parallel", "parallel", "arbitrary"),
                                             vmem_limit_bytes=VMEM_LIMIT_BYTES),
        name="banded_attention",
    )(sink_arr, src, src, src, src, src, src, src, cos_d, cos_d, cos_d, sin_d, sin_d, sin_d, tile2(q_gain), tile2(k_gain))
    return res if want_lse else res[0]


def _window_attention(proj, cos, sin, q_gain, k_gain, sink, batch, seq):
    qw = SWA_HEADS * HEAD_DIM
    kw = SWA_KV_HEADS * HEAD_DIM
    return _banded_attention(proj, cos, sin, q_gain, k_gain, sink, batch=batch, seq=seq, dil=1, hq=SWA_HEADS,
                             hkv=SWA_KV_HEADS, q_off=0, k_off=qw, v_off=qw + kw, window=SWA_WINDOW, want_lse=False)


def _merge_kernel(*refs):
    n = len(DIL_PATTERNS)
    o_refs, l_refs, out_ref = refs[:n], refs[n:2 * n], refs[2 * n]
    lses = [r[...] for r in l_refs]
    m = functools.reduce(jnp.maximum, lses)
    es = [jnp.exp(l - m) for l in lses]
    den = functools.reduce(lambda a, b: a + b, es)
    out_ref[...] = functools.reduce(lambda a, b: a + b, [(e / den) * o[...] for e, o in zip(es, o_refs)])


def _dilated_attention(proj, cos, sin, q_gain, k_gain, batch, seq):
    t = batch * seq
    gw = DIL_HEADS_PER_GROUP * HEAD_DIM
    full = DIL_HEADS * HEAD_DIM
    outs, lses = [], []
    for gi, (win, dil) in enumerate(DIL_PATTERNS):
        o, lse = _banded_attention(proj.reshape(t // dil, dil * proj.shape[1]), cos, sin, q_gain, k_gain, None,
                                   batch=batch, seq=seq, dil=dil, hq=DIL_HEADS_PER_GROUP, hkv=DIL_HEADS_PER_GROUP,
                                   q_off=gi * gw, k_off=full + gi * gw, v_off=2 * full + gi * gw,
                                   window=win // (2 * dil), want_lse=True)
        outs.append(o.reshape(t, gw))
        lses.append(lse.reshape(t, gw))
    tm = 1024
    spec = pl.BlockSpec((tm, gw), lambda i: (i, 0))
    return pl.pallas_call(
        _merge_kernel, grid=(t // tm,), in_specs=[spec] * (2 * len(DIL_PATTERNS)), out_specs=spec,
        out_shape=jax.ShapeDtypeStruct((t, gw), jnp.float32),
        compiler_params=pltpu.CompilerParams(dimension_semantics=("parallel",)), name="dilated_merge",
    )(*outs, *lses)


HALO = SUBLANES
DT_LANES = LANES


def _conv_silu(prev_ref, cur_ref, next_ref, w_ref, b_ref, has_prev, has_next):
    prev = jnp.where(has_prev, prev_ref[...], 0.0)
    nxt = jnp.where(has_next, next_ref[...], 0.0)
    cat = jnp.concatenate([prev, cur_ref[...], nxt], axis=0)
    w = w_ref[...]
    acc = b_ref[...]
    for k in range(CONV_WIDTH):
        start = HALO + k - CONV_WIDTH // 2
        acc = acc + cat[start:start + SSM_CHUNK, :] * w[k:k + 1, :]
    return acc * jax.nn.sigmoid(acc)


def _lane_col(a, j):
    return jnp.broadcast_to(a[:, j:j + 1], a.shape)


def _ssd_kernel(xp_ref, xc_ref, xn_ref, bp_ref, bc_ref, bn_ref, dt_ref, z_ref, part_ref, wx_ref, bx_ref, wb_ref,
                bb_ref, dtb_ref, aneg_ref, skip_ref, nw_ref, o_ref, h_ref, *, rev):
    c = pl.program_id(1)
    nchunk = pl.num_programs(1)
    first, last = c == 0, c == nchunk - 1
    has_prev = jnp.logical_not(last) if rev else jnp.logical_not(first)
    has_next = jnp.logical_not(first) if rev else jnp.logical_not(last)

    @pl.when(first)
    def _():
        h_ref[...] = jnp.zeros_like(h_ref)

    xs = _conv_silu(xp_ref, xc_ref, xn_ref, wx_ref, bx_ref, has_prev, has_next)
    bc = _conv_silu(bp_ref, bc_ref, bn_ref, wb_ref, bb_ref, has_prev, has_next)
    gn = SSM_GROUPS * SSM_STATE
    bm, cm = bc[:, :gn], bc[:, gn:]

    dt_raw = dt_ref[...]
    if rev:
        dt_raw = pltpu.roll(dt_raw, DT_LANES - SSM_HEADS, axis=1)
    xb = dt_raw + dtb_ref[...]
    dt = jnp.maximum(xb, 0.0) + jnp.log(1.0 + jnp.exp(-jnp.abs(xb)))
    da = dt * aneg_ref[...]

    row = lax.broadcasted_iota(jnp.int32, (SSM_CHUNK, SSM_CHUNK), 0)
    col = lax.broadcasted_iota(jnp.int32, (SSM_CHUNK, SSM_CHUNK), 1)
    keep = (col >= row) if rev else (col <= row)
    cs = jnp.dot(keep.astype(jnp.float32), da, precision=lax.Precision.HIGHEST,
                 preferred_element_type=jnp.float32)
    cs_t = cs.T
    end_row = 0 if rev else SSM_CHUNK - 1

    lane = lax.broadcasted_iota(jnp.int32, (SSM_CHUNK, LANES), 1)
    low = lane < SSM_HEAD_DIM
    nt = (((1,), (1,)), ((), ()))
    bf = jnp.bfloat16
    heads_per_group = SSM_HEADS // SSM_GROUPS
    y_chunks = []
    for pair in range(SSM_HEADS // 2):
        g = (2 * pair) // heads_per_group
        bm_g = bm[:, g * SSM_STATE:(g + 1) * SSM_STATE]
        cm_g = cm[:, g * SSM_STATE:(g + 1) * SSM_STATE]
        if pair % (heads_per_group // 2) == 0:
            cb = lax.dot_general(cm_g.astype(bf), bm_g.astype(bf), nt, preferred_element_type=jnp.float32)
            bm_t = bm_g.T.astype(bf)
        xs_c = xs[:, pair * LANES:(pair + 1) * LANES]
        per_head = []
        for j in (2 * pair, 2 * pair + 1):
            ccol = _lane_col(cs, j)
            dec = jnp.where(keep, jnp.exp(jnp.where(keep, ccol - cs_t[j:j + 1, :], 0.0)), 0.0)
            total = ccol[end_row:end_row + 1, :]
            per_head.append(((cb * dec).astype(bf), _lane_col(dt, j), jnp.exp(ccol), jnp.exp(total - ccol),
                             jnp.exp(total)))
        (m_a, d_a, e_a, t_a, g_a), (m_b, d_b, e_b, t_b, g_b) = per_head
        xdt = xs_c * jnp.where(low, d_a, d_b)
        xdt_b = xdt.astype(bf)
        y_diag = jnp.where(low, jnp.dot(m_a, xdt_b, preferred_element_type=jnp.float32),
                           jnp.dot(m_b, xdt_b, preferred_element_type=jnp.float32))
        h = h_ref[pair]
        y_off = jnp.dot(cm_g.astype(bf), h.astype(bf), preferred_element_type=jnp.float32) * jnp.where(low, e_a, e_b)
        states = jnp.dot(bm_t, (xdt * jnp.where(low, t_a, t_b)).astype(bf), preferred_element_type=jnp.float32)
        h_ref[pair] = h * jnp.where(low, g_a, g_b) + states
        y = y_diag + y_off
        if not rev:
            y = y + xs_c * skip_ref[:, pair * LANES:(pair + 1) * LANES]
        y_chunks.append(y)
    y = jnp.concatenate(y_chunks, axis=1)
    if not rev:
        o_ref[...] = y
    else:
        z = z_ref[...]
        y = (y + part_ref[...]) * (z * jax.nn.sigmoid(z))
        o_ref[...] = y * lax.rsqrt(jnp.mean(y * y, axis=-1, keepdims=True) + EPS) * nw_ref[...]


def _ssd_mixer(proj, p, batch, seq):
    nchunk = seq // SSM_CHUNK
    assert seq % SSM_CHUNK == 0
    f32 = jnp.float32
    halo_per_chunk = SSM_CHUNK // HALO
    xcol = 1
    bc_w = 2 * SSM_GROUPS * SSM_STATE
    bc_col = (2 * SSM_INNER) // bc_w
    dt_col = (2 * SSM_INNER + bc_w) // DT_LANES
    pad16 = lambda a: jnp.pad(a.astype(f32), (0, LANES - SSM_HEADS)).reshape(1, LANES)
    per_lane = lambda a: jnp.repeat(a.astype(f32), SSM_HEAD_DIM).reshape(1, SSM_INNER)
    wx, wb = p['conv_w'][:, :SSM_INNER].astype(f32), p['conv_w'][:, SSM_INNER:].astype(f32)
    bx, bb = p['conv_b'][:SSM_INNER].reshape(1, -1).astype(f32), p['conv_b'][SSM_INNER:].reshape(1, -1).astype(f32)
    skip = per_lane(p['d_skip'])
    nw = p['ssm_norm_w'].reshape(1, SSM_INNER).astype(f32)

    def run(rev, part):
        pos = (lambda c: nchunk - 1 - c) if rev else (lambda c: c)

        def chunk(w, colblk):
            return pl.BlockSpec((SSM_CHUNK, w), lambda b, c: (b * nchunk + pos(c), colblk))

        def halo(w, colblk, side):
            def index(b, c):
                r = (b * nchunk + pos(c)) * halo_per_chunk
                r = r - 1 if side < 0 else r + halo_per_chunk
                return (jnp.clip(r, 0, batch * nchunk * halo_per_chunk - 1), colblk)
            return pl.BlockSpec((HALO, w), index)

        const = lambda a: pl.BlockSpec(a.shape, lambda b, c: (0, 0))
        dtb = pad16(p['dt_bias_b'] if rev else p['dt_bias_f'])
        aneg = pad16(-jnp.exp((p['a_log_b'] if rev else p['a_log_f']).astype(f32)))
        consts = (wx, bx, wb, bb, dtb, aneg, skip, nw)
        return pl.pallas_call(
            functools.partial(_ssd_kernel, rev=rev),
            grid=(batch, nchunk),
            in_specs=[halo(SSM_INNER, xcol, -1), chunk(SSM_INNER, xcol), halo(SSM_INNER, xcol, 1),
                      halo(bc_w, bc_col, -1), chunk(bc_w, bc_col), halo(bc_w, bc_col, 1),
                      chunk(DT_LANES, dt_col), chunk(SSM_INNER, 0), chunk(SSM_INNER, 0)] + [const(a) for a in consts],
            out_specs=chunk(SSM_INNER, 0),
            out_shape=jax.ShapeDtypeStruct((batch * seq, SSM_INNER), f32),
            scratch_shapes=[pltpu.VMEM((SSM_HEADS // 2, SSM_STATE, LANES), f32)],
            compiler_params=pltpu.CompilerParams(dimension_semantics=("parallel", "arbitrary"),
                                                 vmem_limit_bytes=VMEM_LIMIT_BYTES),
            name="ssd_bwd" if rev else "ssd_fwd",
        )(proj, proj, proj, proj, proj, proj, proj, proj, part, *consts)

    return run(True, run(False, proj))


ROUTE_TM = 256
CAND_GROUPS = ((0, 16), (1, 8), (2, 8), (3, 8), (4, 8), (5, 8), (6, 8), (7, 8))
CAND_ROWS = sum(n for _, n in CAND_GROUPS) + SUBLANES
NO_PAIR = PEER_TOPK * PEER_TOPK
TOPK_SHIFT = PEER_TOPK.bit_length() - 1
assert 1 << TOPK_SHIFT == PEER_TOPK


def _cand_flat_ids():
    flat = []
    for a, n in CAND_GROUPS:
        flat += [a * PEER_TOPK + b if (a + 1) * (b + 1) <= PEER_TOPK else NO_PAIR for b in range(n)]
    flat += [a * PEER_TOPK for a in range(SUBLANES, PEER_TOPK)]
    return jnp.broadcast_to(jnp.array(flat, jnp.int32)[:, None], (CAND_ROWS, LANES))


def _top_rows(s, row_ids, k, sentinel):
    slot = lax.broadcasted_iota(jnp.int32, (k, LANES), 0)
    vals = jnp.zeros((k, LANES), jnp.float32)
    ids = jnp.zeros((k, LANES), jnp.int32)
    for r in range(k):
        m = jnp.max(s, axis=0, keepdims=True)
        i = jnp.min(jnp.where(s == m, row_ids, sentinel), axis=0, keepdims=True)
        vals = jnp.where(slot == r, m, vals)
        ids = jnp.where(slot == r, i, ids)
        s = jnp.where(row_ids == i, -jnp.inf, s)
    return vals, ids


def _lookup_rows(table, pos):
    slot = lax.broadcasted_iota(jnp.int32, table.shape, 0)
    return jnp.sum(jnp.where(slot == pos, table, 0), axis=0, keepdims=True)


def _peer_route_kernel(x_ref, g_ref, wq_ref, k1_ref, k2_ref, flat_ref, xn_ref, idx_ref, gate_ref, xb_ref):
    nt = (((1,), (1,)), ((), ()))

    @pl.when(pl.program_id(1) == 0)
    def _():
        x = x_ref[...]
        xn = x * lax.rsqrt(jnp.mean(x * x, axis=-1, keepdims=True) + EPS) * g_ref[...]
        xn_ref[...] = xn
        xb_ref[...] = xn.astype(jnp.bfloat16)

    q = jnp.dot(xb_ref[...], wq_ref[...], preferred_element_type=jnp.float32).astype(jnp.bfloat16)
    half = PEER_QDIM // 2
    s1 = lax.dot_general(k1_ref[...], q[:, :half], nt, preferred_element_type=jnp.float32)
    s2 = lax.dot_general(k2_ref[...], q[:, half:], nt, preferred_element_type=jnp.float32)
    key_ids = lax.broadcasted_iota(jnp.int32, (PEER_N_KEYS, LANES), 0)
    flat = flat_ref[...]
    for lt in range(ROUTE_TM // LANES):
        lanes = slice(lt * LANES, (lt + 1) * LANES)
        v1, i1 = _top_rows(s1[:, lanes], key_ids, PEER_TOPK, PEER_N_KEYS)
        v2, i2 = _top_rows(s2[:, lanes], key_ids, PEER_TOPK, PEER_N_KEYS)
        blocks = [v1[a:a + 1] + v2[:n] for a, n in CAND_GROUPS] + [v1[SUBLANES:] + v2[0:1]]
        cand = jnp.where(flat < NO_PAIR, jnp.concatenate(blocks, axis=0), -jnp.inf)
        top_s, top_flat = _top_rows(cand, flat, PEER_TOPK, 2 * NO_PAIR)
        ids = jnp.concatenate(
            [_lookup_rows(i1, top_flat[r:r + 1] >> TOPK_SHIFT) * PEER_N_KEYS + _lookup_rows(i2, top_flat[r:r + 1] & (PEER_TOPK - 1))
             for r in range(PEER_TOPK)], axis=0)
        e = jnp.exp(top_s - top_s[0:1])
        idx_ref[:, lanes] = ids * PACK_ROWS
        gate_ref[:, lanes] = e / jnp.sum(e, axis=0, keepdims=True)


def _peer_route(x, gain, wq, k1, k2):
    t = x.shape[0]
    assert t % ROUTE_TM == 0
    xn, idx_t, gate_t = pl.pallas_call(
        _peer_route_kernel,
        grid=(t // ROUTE_TM, PEER_HEADS),
        in_specs=[pl.BlockSpec((ROUTE_TM, D_MODEL), lambda i, h: (i, 0)),
                  pl.BlockSpec((1, D_MODEL), lambda i, h: (0, 0)),
                  pl.BlockSpec((D_MODEL, PEER_QDIM), lambda i, h: (0, h)),
                  pl.BlockSpec((PEER_N_KEYS, PEER_QDIM // 2), lambda i, h: (0, 0)),
                  pl.BlockSpec((PEER_N_KEYS, PEER_QDIM // 2), lambda i, h: (0, 0)),
                  pl.BlockSpec((CAND_ROWS, LANES), lambda i, h: (0, 0))],
        out_specs=[pl.BlockSpec((ROUTE_TM, D_MODEL), lambda i, h: (i, 0)),
                   pl.BlockSpec((PEER_TOPK, ROUTE_TM), lambda i, h: (h, i)),
                   pl.BlockSpec((PEER_TOPK, ROUTE_TM), lambda i, h: (h, i))],
        out_shape=[jax.ShapeDtypeStruct((t, D_MODEL), jnp.float32),
                   jax.ShapeDtypeStruct((PEER_PICKS, t), jnp.int32),
                   jax.ShapeDtypeStruct((PEER_PICKS, t), jnp.float32)],
        scratch_shapes=[pltpu.VMEM((ROUTE_TM, D_MODEL), jnp.bfloat16)],
        compiler_params=pltpu.CompilerParams(dimension_semantics=("parallel", "arbitrary"),
                                             vmem_limit_bytes=VMEM_LIMIT_BYTES),
        name="peer_route",
    )(x, gain.reshape(1, D_MODEL), wq.astype(jnp.bfloat16), k1.astype(jnp.bfloat16), k2.astype(jnp.bfloat16),
      _cand_flat_ids())
    return xn, idx_t.T, gate_t.T


def _encoder_layer(x, p):
    b, l, _ = x.shape
    t = b * l
    xt = x.reshape(t, D_MODEL)
    w_in, gain = p['w_in'], p['norm1_w']
    ssm_w, swa_w, dil_w = sum(IN_SPLITS[:6]), sum(IN_SPLITS[6:9]), sum(IN_SPLITS[9:12])
    ssm_proj = _matmul(xt, w_in[:, :ssm_w], gain=gain, tn=384, keep_pad=True)
    swa_proj = _matmul(xt, w_in[:, ssm_w:ssm_w + swa_w], gain=gain, tn=768)
    dil_proj = _matmul(xt, w_in[:, ssm_w + swa_w:ssm_w + swa_w + dil_w], gain=gain, tn=768)
    gate_logits = _matmul(xt, w_in[:, ssm_w + swa_w + dil_w:], gain=gain, tn=1024)
    y_a = _ssd_mixer(ssm_proj, p, b, l)
    cos, sin = _rope_tables(l)
    y_b = _window_attention(swa_proj, cos, sin, p['q_norm_b'], p['k_norm_b'], p['sink_b'], b, l)
    y_c = _dilated_attention(dil_proj, cos, sin, p['q_norm_c'], p['k_norm_c'], b, l)
    xt = _mixer_out(xt, y_a, y_b, y_c, gate_logits, p['w_a'], p['w_b'], p['w_c'], p['w_out'])
    xn, idx, gate = _peer_route(xt, p['norm2_w'], p['peer_wq'], p['peer_k1'], p['peer_k2'])
    xt = _peer_experts(xt, xn, idx, gate, p['u_packed'], p['v_packed'])
    return xt.reshape(b, l, D_MODEL)


_LAYER_KEYS = ('norm1_w', 'w_in', 'conv_w', 'conv_b', 'dt_bias_f', 'dt_bias_b', 'a_log_f', 'a_log_b', 'd_skip',
               'ssm_norm_w', 'w_a', 'q_norm_b', 'k_norm_b', 'sink_b', 'w_b', 'q_norm_c', 'k_norm_c', 'w_c', 'w_out',
               'norm2_w', 'peer_wq', 'peer_k1', 'peer_k2', 'peer_u', 'peer_v')


def kernel(x_prompt, x_sample, norm1_w, w_in, conv_w, conv_b, dt_bias_f, dt_bias_b, a_log_f, a_log_b, d_skip, ssm_norm_w, w_a, q_norm_b, k_norm_b, sink_b, w_b, q_norm_c, k_norm_c, w_c, w_out, norm2_w, peer_wq, peer_k1, peer_k2, peer_u, peer_v):
    stacked = dict(zip(_LAYER_KEYS, (norm1_w, w_in, conv_w, conv_b, dt_bias_f, dt_bias_b, a_log_f, a_log_b, d_skip,
                                     ssm_norm_w, w_a, q_norm_b, k_norm_b, sink_b, w_b, q_norm_c, k_norm_c, w_c, w_out,
                                     norm2_w, peer_wq, peer_k1, peer_k2, peer_u, peer_v)))
    layers = []
    for i in range(w_in.shape[0]):
        p = {k: v[i] for k, v in stacked.items()}
        p['u_packed'] = _pack_table(p['peer_u'])
        p['v_packed'] = _pack_table(p['peer_v'])
        layers.append(p)

    def trunk(x):
        for p in layers:
            x = _encoder_layer(x, p)
        return x

    return trunk(x_prompt), trunk(x_sample)
```

```python
import functools

import jax
import jax.numpy as jnp
from jax import lax
from jax.experimental import pallas as pl
from jax.experimental.pallas import tpu as pltpu

D_MODEL = 1024
HEAD_DIM = 64
SSM_HEADS = 16
SSM_HEAD_DIM = 64
SSM_INNER = SSM_HEADS * SSM_HEAD_DIM
SSM_GROUPS = 2
SSM_STATE = 128
SSM_CHUNK = 128
CONV_WIDTH = 5
SWA_HEADS = 16
SWA_KV_HEADS = 4
SWA_WINDOW = 128
DIL_PATTERNS = ((128, 1), (512, 4), (2048, 16))
DIL_HEADS_PER_GROUP = 4
DIL_HEADS = DIL_HEADS_PER_GROUP * 3
PEER_HEADS = 8
PEER_N_KEYS = 128
PEER_EXPERTS = PEER_N_KEYS * PEER_N_KEYS
PEER_TOPK = 16
PEER_QDIM = 256
PEER_PICKS = PEER_HEADS * PEER_TOPK
N_BRANCHES = 3
ROPE_THETA = 10000.0
EPS = 1e-6
NEG_INF = -1e30
IN_SPLITS = (SSM_INNER, SSM_INNER, SSM_GROUPS * SSM_STATE, SSM_GROUPS * SSM_STATE, SSM_HEADS, SSM_HEADS,
             SWA_HEADS * HEAD_DIM, SWA_KV_HEADS * HEAD_DIM, SWA_KV_HEADS * HEAD_DIM,
             DIL_HEADS * HEAD_DIM, DIL_HEADS * HEAD_DIM, DIL_HEADS * HEAD_DIM,
             N_BRANCHES * D_MODEL)

LANES = 128
SUBLANES = 8
VMEM_LIMIT_BYTES = 56 * 1024 * 1024

PACK_WORDS = D_MODEL // 2
PACK_ROWS = PACK_WORDS // LANES
ROW_SUBLANES = D_MODEL // LANES
PEER_TB = 32


def _mm_kernel(x_ref, g_ref, w_ref, o_ref, xn_ref, *, norm):
    @pl.when(pl.program_id(1) == 0)
    def _():
        x = x_ref[...].astype(jnp.float32)
        if norm:
            x = x * lax.rsqrt(jnp.mean(x * x, axis=-1, keepdims=True) + EPS) * g_ref[...]
        xn_ref[...] = x.astype(jnp.bfloat16)

    o_ref[...] = jnp.dot(xn_ref[...], w_ref[...], preferred_element_type=jnp.float32)


def _matmul(x, w, gain=None, tm=1024, tn=512, keep_pad=False):
    m, k = x.shape
    n = w.shape[1]
    tm = min(tm, m)
    tn = min(tn, n)
    assert m % tm == 0
    n_pad = -n % tn
    wb = w.astype(jnp.bfloat16)
    if n_pad:
        wb = jnp.pad(wb, ((0, 0), (0, n_pad)))
    g = jnp.ones((1, k), jnp.float32) if gain is None else gain.reshape(1, k).astype(jnp.float32)
    out = pl.pallas_call(
        functools.partial(_mm_kernel, norm=gain is not None),
        grid=(m // tm, (n + n_pad) // tn),
        in_specs=[pl.BlockSpec((tm, k), lambda i, j: (i, 0)),
                  pl.BlockSpec((1, k), lambda i, j: (0, 0)),
                  pl.BlockSpec((k, tn), lambda i, j: (0, j))],
        out_specs=pl.BlockSpec((tm, tn), lambda i, j: (i, j)),
        out_shape=jax.ShapeDtypeStruct((m, n + n_pad), jnp.float32),
        scratch_shapes=[pltpu.VMEM((tm, k), jnp.bfloat16)],
        compiler_params=pltpu.CompilerParams(dimension_semantics=("parallel", "arbitrary"),
                                             vmem_limit_bytes=VMEM_LIMIT_BYTES),
        name="proj_matmul",
    )(x, g, wb)
    return out[:, :n] if n_pad and not keep_pad else out


def _mixer_out_kernel(x_ref, ya_ref, yb_ref, yc_ref, gl_ref, wa_ref, wb_ref, wc_ref, wo_ref, o_ref):
    def proj(y_ref, w_ref):
        return jnp.dot(y_ref[...].astype(jnp.bfloat16), w_ref[...], preferred_element_type=jnp.float32)

    gates = jax.nn.sigmoid(gl_ref[...])
    merged = (gates[:, :D_MODEL] * proj(ya_ref, wa_ref) + gates[:, D_MODEL:2 * D_MODEL] * proj(yb_ref, wb_ref)
              + gates[:, 2 * D_MODEL:] * proj(yc_ref, wc_ref))
    o_ref[...] = x_ref[...] + jnp.dot(merged.astype(jnp.bfloat16), wo_ref[...], preferred_element_type=jnp.float32)


def _mixer_out(x, ya, yb, yc, gate_logits, w_a, w_b, w_c, w_out, tm=512):
    t = x.shape[0]
    assert t % tm == 0
    ws = [w.astype(jnp.bfloat16) for w in (w_a, w_b, w_c, w_out)]
    row = lambda a: pl.BlockSpec((tm, a.shape[1]), lambda i: (i, 0))
    full = lambda a: pl.BlockSpec(a.shape, lambda i: (0, 0))
    acts = (x, ya, yb, yc, gate_logits)
    return pl.pallas_call(
        _mixer_out_kernel,
        grid=(t // tm,),
        in_specs=[row(a) for a in acts] + [full(w) for w in ws],
        out_specs=row(x),
        out_shape=jax.ShapeDtypeStruct(x.shape, jnp.float32),
        compiler_params=pltpu.CompilerParams(dimension_semantics=("parallel",), vmem_limit_bytes=VMEM_LIMIT_BYTES),
        name="mixer_out",
    )(*acts, *ws)


def _pack_table(tab):
    e = tab.shape[0]
    tb = lax.bitcast_convert_type(tab.astype(jnp.bfloat16), jnp.uint16).astype(jnp.uint32)
    tb = tb.reshape(e, PACK_ROWS, 2, LANES)
    packed = tb[:, :, 0, :] | (tb[:, :, 1, :] << 16)
    return lax.bitcast_convert_type(packed, jnp.int32).reshape(e * PACK_ROWS, LANES)


def _peer_constants():
    col = jnp.arange(PEER_PICKS * ROW_SUBLANES)
    diag = (col[None, :] % ROW_SUBLANES == jnp.arange(ROW_SUBLANES)[:, None]).astype(jnp.float32)
    group = (col[:, None] // ROW_SUBLANES == jnp.arange(PEER_PICKS)[None, :]).astype(jnp.bfloat16)
    return diag, group, group.T


def _split2(x):
    hi = x.astype(jnp.bfloat16).astype(jnp.float32)
    return hi, x - hi


def _stack_pieces(pieces):
    return jnp.concatenate(list(pieces), axis=1).astype(jnp.bfloat16)


def _sum_pieces(y):
    return y[:ROW_SUBLANES] + y[ROW_SUBLANES:]


GATHER_BATCH = 16


def _gather_rows(idx_ref, tab_ref, t, raw_ref):
    for j0 in range(0, PEER_PICKS, GATHER_BATCH):
        rows = [tab_ref[pl.ds(pl.multiple_of(idx_ref[t, j], PACK_ROWS), PACK_ROWS), :]
                for j in range(j0, j0 + GATHER_BATCH)]
        for k in range(0, GATHER_BATCH, 2):
            raw_ref[pl.ds((j0 + k) * PACK_ROWS, 2 * PACK_ROWS), :] = jnp.concatenate(rows[k:k + 2], axis=0)


def _token_rows(t):
    return pl.ds(t * ROW_SUBLANES, ROW_SUBLANES)


def _pipelined_tokens(gather, compute, bufs):
    gather(0, bufs[0])
    for t in range(PEER_TB):
        compute(t, bufs[t % 2])
        if t + 1 < PEER_TB:
            gather(t + 1, bufs[(t + 1) % 2])


def _with_staged_ids(idx_hbm, idx_smem, sem, body):
    i = pl.program_id(0)
    n = pl.num_programs(0)

    def copy(step, slot):
        return pltpu.make_async_copy(idx_hbm.at[pl.ds(step * PEER_TB, PEER_TB)], idx_smem.at[slot], sem.at[slot])

    @pl.when(i == 0)
    def _():
        copy(0, 0).start()

    for slot in range(2):
        @pl.when(i % 2 == slot)
        def _():
            @pl.when(i + 1 < n)
            def _():
                copy(i + 1, 1 - slot).start()

            copy(i, slot).wait()
            body(idx_smem.at[slot])


def _peer_act_kernel(idx_hbm, x_ref, gate_ref, diag_ref, group_ref, tab_ref, w_ref, raw0_ref, raw1_ref, x3_ref,
                     picked_ref, idx_smem, sem):
    nt = (((1,), (1,)), ((), ()))
    x3_ref[...] = _stack_pieces(_split2(x_ref[...].reshape(PEER_TB, ROW_SUBLANES, LANES)))

    def compute(t, buf_ref):
        wmat = pltpu.bitcast(buf_ref[...], jnp.bfloat16)
        scores = _sum_pieces(lax.dot_general(x3_ref[t], wmat, nt, preferred_element_type=jnp.float32))
        picked_ref[_token_rows(t), :] = scores * diag_ref[...]

    _with_staged_ids(idx_hbm, idx_smem, sem, lambda ids: _pipelined_tokens(
        functools.partial(_gather_rows, ids, tab_ref), compute, (raw0_ref, raw1_ref)))

    act8 = sum(jnp.dot(piece.astype(jnp.bfloat16), group_ref[...], preferred_element_type=jnp.float32)
               for piece in _split2(picked_ref[...]))
    act = jnp.sum(act8.reshape(PEER_TB, ROW_SUBLANES, PEER_PICKS), axis=1)
    gelu = 0.5 * act * (1.0 + lax.erf(act * (2.0 ** -0.5)))
    w_ref[...] = gate_ref[...] * gelu


def _peer_out_kernel(idx_hbm, w_ref, res_ref, diag_ref, expand_ref, tab_ref, o_ref, raw0_ref, raw1_ref, lhs_ref,
                     idx_smem, sem):
    wide = [jnp.dot(piece.astype(jnp.bfloat16), expand_ref[...], preferred_element_type=jnp.float32)
            for piece in _split2(w_ref[...])]
    diag = diag_ref[...]
    for t in range(PEER_TB):
        rows = [jnp.broadcast_to(wd[t:t + 1], diag.shape) * diag for wd in wide]
        lhs_ref[t] = jnp.concatenate(rows, axis=0).astype(jnp.bfloat16)

    def compute(t, buf_ref):
        wmat = pltpu.bitcast(buf_ref[...], jnp.bfloat16)
        rows = _token_rows(t)
        o_ref[rows, :] = res_ref[rows, :] + _sum_pieces(jnp.dot(lhs_ref[t], wmat, preferred_element_type=jnp.float32))

    _with_staged_ids(idx_hbm, idx_smem, sem, lambda ids: _pipelined_tokens(
        functools.partial(_gather_rows, ids, tab_ref), compute, (raw0_ref, raw1_ref)))


def _const_spec(shape):
    return pl.BlockSpec(shape, lambda i: (0,) * len(shape), pipeline_mode=pl.Buffered(1))


def _peer_experts(x_res, xn, idx, gate, u_packed, v_packed):
    t = xn.shape[0]
    assert t % PEER_TB == 0 and PEER_TB % 2 == 0
    grid = (t // PEER_TB,)
    params = pltpu.CompilerParams(dimension_semantics=("arbitrary",), vmem_limit_bytes=VMEM_LIMIT_BYTES)
    ids_spec = pl.BlockSpec(memory_space=pl.ANY)
    ids_scratch = [pltpu.SMEM((2, PEER_TB, PEER_PICKS), jnp.int32), pltpu.SemaphoreType.DMA((2,))]
    rows_spec = pl.BlockSpec((PEER_TB * ROW_SUBLANES, LANES), lambda i: (i, 0))
    picks_spec = pl.BlockSpec((PEER_TB, PEER_PICKS), lambda i: (i, 0))
    raw = pltpu.VMEM((PEER_PICKS * PACK_ROWS, LANES), jnp.int32)
    diag, group, expand = _peer_constants()
    stacked = 2 * ROW_SUBLANES

    w = pl.pallas_call(
        _peer_act_kernel,
        grid=grid,
        in_specs=[ids_spec, rows_spec, picks_spec, _const_spec(diag.shape), _const_spec(group.shape),
                  _const_spec(u_packed.shape)],
        out_specs=picks_spec,
        out_shape=jax.ShapeDtypeStruct((t, PEER_PICKS), jnp.float32),
        scratch_shapes=[raw, raw, pltpu.VMEM((PEER_TB, stacked, LANES), jnp.bfloat16),
                        pltpu.VMEM((PEER_TB * ROW_SUBLANES, PEER_PICKS * ROW_SUBLANES), jnp.float32)] + ids_scratch,
        compiler_params=params,
        name="peer_act",
    )(idx, xn.reshape(t * ROW_SUBLANES, LANES), gate, diag, group, u_packed)

    out = pl.pallas_call(
        _peer_out_kernel,
        grid=grid,
        in_specs=[ids_spec, picks_spec, rows_spec, _const_spec(diag.shape), _const_spec(expand.shape),
                  _const_spec(v_packed.shape)],
        out_specs=rows_spec,
        out_shape=jax.ShapeDtypeStruct((t * ROW_SUBLANES, LANES), jnp.float32),
        scratch_shapes=[raw, raw, pltpu.VMEM((PEER_TB, stacked, PEER_PICKS * ROW_SUBLANES), jnp.bfloat16)]
        + ids_scratch,
        compiler_params=params,
        name="peer_out",
    )(idx, w, x_res.reshape(t * ROW_SUBLANES, LANES), diag, expand, v_packed)
    return out.reshape(t, D_MODEL)


ATT_BLK = 128
HALF_HEAD = HEAD_DIM // 2


def _rope_tables(seq):
    inv_freq = ROPE_THETA ** (-jnp.arange(HALF_HEAD, dtype=jnp.float32) / HALF_HEAD)
    ang = jnp.arange(seq).astype(jnp.float32)[:, None] * inv_freq[None, :]
    cos, sin = jnp.cos(ang), jnp.sin(ang)
    return jnp.tile(jnp.concatenate([cos, cos], axis=1), (1, 2)), jnp.tile(jnp.concatenate([-sin, sin], axis=1), (1, 2))


def _norm_rope(a, gain, cos, sin):
    lane = lax.broadcasted_iota(jnp.int32, (a.shape[0], LANES), 1)
    low_head = lane < HEAD_DIM
    first_half = (lane % HEAD_DIM) < HALF_HEAD
    out = []
    for c in range(a.shape[1] // LANES):
        x = a[:, c * LANES:(c + 1) * LANES]
        sq = x * x
        lo = jnp.sum(jnp.where(low_head, sq, 0.0), axis=1, keepdims=True)
        hi = jnp.sum(jnp.where(low_head, 0.0, sq), axis=1, keepdims=True)
        y = x * lax.rsqrt(jnp.where(low_head, lo, hi) * (1.0 / HEAD_DIM) + EPS) * gain
        rot = jnp.where(first_half, pltpu.roll(y, LANES - HALF_HEAD, axis=1), pltpu.roll(y, HALF_HEAD, axis=1))
        out.append(y * cos + rot * sin)
    return out


def _attn_kernel(sink_ref, q_ref, kp_ref, kc_ref, kn_ref, vp_ref, vc_ref, vn_ref, cp_ref, cc_ref, cn_ref,
                 sp_ref, sc_ref, sn_ref, qg_ref, kg_ref, o_ref, *lse_refs, hq, hkv, window, use_sink):
    n = pl.program_id(2)
    nblk = pl.num_programs(2)
    nt = (((1,), (1,)), ((), ()))
    scale = HEAD_DIM ** -0.5
    q_chunks = _norm_rope(q_ref[...], qg_ref[...], cc_ref[...], sc_ref[...])
    k_parts = [_norm_rope(k_ref[...], kg_ref[...], c_ref[...], s_ref[...])
               for k_ref, c_ref, s_ref in ((kp_ref, cp_ref, sp_ref), (kc_ref, cc_ref, sc_ref), (kn_ref, cn_ref, sn_ref))]
    k_chunks = [jnp.concatenate([part[c] for part in k_parts], axis=0).astype(jnp.bfloat16)
                for c in range(hkv // 2)]
    v_all = jnp.concatenate([vp_ref[...], vc_ref[...], vn_ref[...]], axis=0).astype(jnp.bfloat16)

    side = kp_ref.shape[0]
    keys = ATT_BLK + 2 * side
    row = lax.broadcasted_iota(jnp.int32, (ATT_BLK, keys), 0)
    col = lax.broadcasted_iota(jnp.int32, (ATT_BLK, keys), 1)
    rel = col - side - row
    first_col = jnp.where(n > 0, 0, side)
    end_col = jnp.where(n < nblk - 1, keys, side + ATT_BLK)
    mask = (jnp.abs(rel) <= window) & (col >= first_col) & (col < end_col)
    lane = lax.broadcasted_iota(jnp.int32, (ATT_BLK, LANES), 1)
    halves = (lane < HEAD_DIM, lane >= HEAD_DIM)

    group = hq // hkv
    outs = [jnp.zeros((ATT_BLK, LANES), jnp.float32) for _ in range(hq // 2)]
    lses = [jnp.zeros((ATT_BLK, LANES), jnp.float32) for _ in range(hq // 2)]
    for h in range(hq):
        g = h // group
        qh = q_chunks[h // 2]
        if h % 2 != g % 2:
            qh = pltpu.roll(qh, HEAD_DIM, axis=1)
        qh = (jnp.where(halves[g % 2], qh, 0.0)).astype(jnp.bfloat16)
        s = lax.dot_general(qh, k_chunks[g // 2], nt, preferred_element_type=jnp.float32) * scale
        s = jnp.where(mask, s, NEG_INF)
        m = jnp.max(s, axis=1, keepdims=True)
        if use_sink:
            m = jnp.maximum(m, sink_ref[h])
        p = jnp.exp(s - m)
        den = jnp.sum(p, axis=1, keepdims=True)
        if use_sink:
            den = den + jnp.exp(sink_ref[h] - m)
        vg = v_all[:, (g // 2) * LANES:(g // 2 + 1) * LANES]
        o = jnp.dot(p.astype(jnp.bfloat16), vg, preferred_element_type=jnp.float32) / den
        if h % 2 != g % 2:
            o = pltpu.roll(o, HEAD_DIM, axis=1)
        outs[h // 2] = jnp.where(halves[h % 2], o, outs[h // 2])
        if lse_refs:
            lses[h // 2] = jnp.where(halves[h % 2], m + jnp.log(den), lses[h // 2])
    o_ref[...] = jnp.concatenate(outs, axis=1)
    if lse_refs:
        lse_refs[0][...] = jnp.concatenate(lses, axis=1)


def _banded_attention(src, cos, sin, q_gain, k_gain, sink, *, batch, seq, dil, hq, hkv, q_off, k_off, v_off, window,
                      want_lse):
    rows = seq // dil
    assert rows % ATT_BLK == 0 and src.shape[0] == batch * rows and src.shape[1] % dil == 0
    nblk = rows // ATT_BLK
    width = src.shape[1] // dil
    qw, kw = hq * HEAD_DIM, hkv * HEAD_DIM
    assert all((r * width + off) % w == 0 for r in range(dil) for off, w in ((q_off, qw), (k_off, kw), (v_off, kw)))

    side = -(-window // SUBLANES) * SUBLANES
    assert side <= ATT_BLK and ATT_BLK % side == 0
    per_blk = ATT_BLK // side
    nside = rows // side

    def neighbour(n, shift):
        return jnp.clip(n * per_blk - 1 if shift < 0 else (n + 1) * per_blk, 0, nside - 1)

    def spec(w, off, shift):
        col = lambda r: (r * width + off) // w
        if shift == 0:
            return pl.BlockSpec((ATT_BLK, w), lambda b, r, n: (b * nblk + n, col(r)))
        return pl.BlockSpec((side, w), lambda b, r, n: (b * nside + neighbour(n, shift), col(r)))

    def table_spec(shift):
        if shift == 0:
            return pl.BlockSpec((ATT_BLK, LANES), lambda b, r, n: (n, r))
        return pl.BlockSpec((side, LANES), lambda b, r, n: (neighbour(n, shift), r))

    tile2 = lambda gain: jnp.tile(gain.astype(jnp.float32), 2).reshape(1, LANES)
    gain_spec = pl.BlockSpec((1, LANES), lambda b, r, n: (0, 0))
    out_spec = pl.BlockSpec((ATT_BLK, qw), lambda b, r, n: (b * nblk + n, r))
    out_shape = jax.ShapeDtypeStruct((batch * rows, dil * qw), jnp.float32)
    use_sink = sink is not None
    sink_arr = sink.astype(jnp.float32) if use_sink else jnp.zeros((hq,), jnp.float32)
    cos_d, sin_d = cos.reshape(rows, dil * LANES), sin.reshape(rows, dil * LANES)
    res = pl.pallas_call(
        functools.partial(_attn_kernel, hq=hq, hkv=hkv, window=window, use_sink=use_sink),
        grid=(batch, dil, nblk),
        in_specs=[pl.BlockSpec(memory_space=pltpu.SMEM), spec(qw, q_off, 0),
                  spec(kw, k_off, -1), spec(kw, k_off, 0), spec(kw, k_off, 1),
                  spec(kw, v_off, -1), spec(kw, v_off, 0), spec(kw, v_off, 1),
                  table_spec(-1), table_spec(0), table_spec(1), table_spec(-1), table_spec(0), table_spec(1),
                  gain_spec, gain_spec],
        out_specs=[out_spec, out_spec] if want_lse else [out_spec],
        out_shape=[out_shape, out_shape] if want_lse else [out_shape],
        compiler_params=pltpu.CompilerParams(dimension_semantics=("parallel", "parallel", "arbitrary"),
                                             vmem_limit_bytes=VMEM_LIMIT_BYTES),
        name="banded_attention",
    )(sink_arr, src, src, src, src, src, src, src, cos_d, cos_d, cos_d, sin_d, sin_d, sin_d, tile2(q_gain), tile2(k_gain))
    return res if want_lse else res[0]


def _window_attention(proj, cos, sin, q_gain, k_gain, sink, batch, seq):
    qw = SWA_HEADS * HEAD_DIM
    kw = SWA_KV_HEADS * HEAD_DIM
    return _banded_attention(proj, cos, sin, q_gain, k_gain, sink, batch=batch, seq=seq, dil=1, hq=SWA_HEADS,
                             hkv=SWA_KV_HEADS, q_off=0, k_off=qw, v_off=qw + kw, window=SWA_WINDOW, want_lse=False)


def _merge_kernel(*refs):
    n = len(DIL_PATTERNS)
    o_refs, l_refs, out_ref = refs[:n], refs[n:2 * n], refs[2 * n]
    lses = [r[...] for r in l_refs]
    m = functools.reduce(jnp.maximum, lses)
    es = [jnp.exp(l - m) for l in lses]
    den = functools.reduce(lambda a, b: a + b, es)
    out_ref[...] = functools.reduce(lambda a, b: a + b, [(e / den) * o[...] for e, o in zip(es, o_refs)])


def _dilated_attention(proj, cos, sin, q_gain, k_gain, batch, seq):
    t = batch * seq
    gw = DIL_HEADS_PER_GROUP * HEAD_DIM
    full = DIL_HEADS * HEAD_DIM
    outs, lses = [], []
    for gi, (win, dil) in enumerate(DIL_PATTERNS):
        o, lse = _banded_attention(proj.reshape(t // dil, dil * proj.shape[1]), cos, sin, q_gain, k_gain, None,
                                   batch=batch, seq=seq, dil=dil, hq=DIL_HEADS_PER_GROUP, hkv=DIL_HEADS_PER_GROUP,
                                   q_off=gi * gw, k_off=full + gi * gw, v_off=2 * full + gi * gw,
                                   window=win // (2 * dil), want_lse=True)
        outs.append(o.reshape(t, gw))
        lses.append(lse.reshape(t, gw))
    tm = 1024
    spec = pl.BlockSpec((tm, gw), lambda i: (i, 0))
    return pl.pallas_call(
        _merge_kernel, grid=(t // tm,), in_specs=[spec] * (2 * len(DIL_PATTERNS)), out_specs=spec,
        out_shape=jax.ShapeDtypeStruct((t, gw), jnp.float32),
        compiler_params=pltpu.CompilerParams(dimension_semantics=("parallel",)), name="dilated_merge",
    )(*outs, *lses)


HALO = SUBLANES
DT_LANES = LANES


def _conv_silu(prev_ref, cur_ref, next_ref, w_ref, b_ref, has_prev, has_next):
    prev = jnp.where(has_prev, prev_ref[...], 0.0)
    nxt = jnp.where(has_next, next_ref[...], 0.0)
    cat = jnp.concatenate([prev, cur_ref[...], nxt], axis=0)
    w = w_ref[...]
    acc = b_ref[...]
    for k in range(CONV_WIDTH):
        start = HALO + k - CONV_WIDTH // 2
        acc = acc + cat[start:start + SSM_CHUNK, :] * w[k:k + 1, :]
    return acc * jax.nn.sigmoid(acc)


def _lane_col(a, j):
    return jnp.broadcast_to(a[:, j:j + 1], a.shape)


def _ssd_kernel(xp_ref, xc_ref, xn_ref, bp_ref, bc_ref, bn_ref, dt_ref, z_ref, part_ref, wx_ref, bx_ref, wb_ref,
                bb_ref, dtb_ref, aneg_ref, skip_ref, nw_ref, o_ref, h_ref, *, rev):
    c = pl.program_id(1)
    nchunk = pl.num_programs(1)
    first, last = c == 0, c == nchunk - 1
    has_prev = jnp.logical_not(last) if rev else jnp.logical_not(first)
    has_next = jnp.logical_not(first) if rev else jnp.logical_not(last)

    @pl.when(first)
    def _():
        h_ref[...] = jnp.zeros_like(h_ref)

    xs = _conv_silu(xp_ref, xc_ref, xn_ref, wx_ref, bx_ref, has_prev, has_next)
    bc = _conv_silu(bp_ref, bc_ref, bn_ref, wb_ref, bb_ref, has_prev, has_next)
    gn = SSM_GROUPS * SSM_STATE
    bm, cm = bc[:, :gn], bc[:, gn:]

    dt_raw = dt_ref[...]
    if rev:
        dt_raw = pltpu.roll(dt_raw, DT_LANES - SSM_HEADS, axis=1)
    xb = dt_raw + dtb_ref[...]
    dt = jnp.maximum(xb, 0.0) + jnp.log(1.0 + jnp.exp(-jnp.abs(xb)))
    da = dt * aneg_ref[...]

    row = lax.broadcasted_iota(jnp.int32, (SSM_CHUNK, SSM_CHUNK), 0)
    col = lax.broadcasted_iota(jnp.int32, (SSM_CHUNK, SSM_CHUNK), 1)
    keep = (col >= row) if rev else (col <= row)
    cs = jnp.dot(keep.astype(jnp.float32), da, precision=lax.Precision.HIGHEST,
                 preferred_element_type=jnp.float32)
    cs_t = cs.T
    end_row = 0 if rev else SSM_CHUNK - 1

    lane = lax.broadcasted_iota(jnp.int32, (SSM_CHUNK, LANES), 1)
    low = lane < SSM_HEAD_DIM
    nt = (((1,), (1,)), ((), ()))
    bf = jnp.bfloat16
    heads_per_group = SSM_HEADS // SSM_GROUPS
    y_chunks = []
    for pair in range(SSM_HEADS // 2):
        g = (2 * pair) // heads_per_group
        bm_g = bm[:, g * SSM_STATE:(g + 1) * SSM_STATE]
        cm_g = cm[:, g * SSM_STATE:(g + 1) * SSM_STATE]
        if pair % (heads_per_group // 2) == 0:
            cb = lax.dot_general(cm_g.astype(bf), bm_g.astype(bf), nt, preferred_element_type=jnp.float32)
            bm_t = bm_g.T.astype(bf)
        xs_c = xs[:, pair * LANES:(pair + 1) * LANES]
        per_head = []
        for j in (2 * pair, 2 * pair + 1):
            ccol = _lane_col(cs, j)
            dec = jnp.where(keep, jnp.exp(jnp.where(keep, ccol - cs_t[j:j + 1, :], 0.0)), 0.0)
            total = ccol[end_row:end_row + 1, :]
            per_head.append(((cb * dec).astype(bf), _lane_col(dt, j), jnp.exp(ccol), jnp.exp(total - ccol),
                             jnp.exp(total)))
        (m_a, d_a, e_a, t_a, g_a), (m_b, d_b, e_b, t_b, g_b) = per_head
        xdt = xs_c * jnp.where(low, d_a, d_b)
        xdt_b = xdt.astype(bf)
        y_diag = jnp.where(low, jnp.dot(m_a, xdt_b, preferred_element_type=jnp.float32),
                           jnp.dot(m_b, xdt_b, preferred_element_type=jnp.float32))
        h = h_ref[pair]
        y_off = jnp.dot(cm_g.astype(bf), h.astype(bf), preferred_element_type=jnp.float32) * jnp.where(low, e_a, e_b)
        states = jnp.dot(bm_t, (xdt * jnp.where(low, t_a, t_b)).astype(bf), preferred_element_type=jnp.float32)
        h_ref[pair] = h * jnp.where(low, g_a, g_b) + states
        y = y_diag + y_off
        if not rev:
            y = y + xs_c * skip_ref[:, pair * LANES:(pair + 1) * LANES]
        y_chunks.append(y)
    y = jnp.concatenate(y_chunks, axis=1)
    if not rev:
        o_ref[...] = y
    else:
        z = z_ref[...]
        y = (y + part_ref[...]) * (z * jax.nn.sigmoid(z))
        o_ref[...] = y * lax.rsqrt(jnp.mean(y * y, axis=-1, keepdims=True) + EPS) * nw_ref[...]


def _ssd_mixer(proj, p, batch, seq):
    nchunk = seq // SSM_CHUNK
    assert seq % SSM_CHUNK == 0
    f32 = jnp.float32
    halo_per_chunk = SSM_CHUNK // HALO
    xcol = 1
    bc_w = 2 * SSM_GROUPS * SSM_STATE
    bc_col = (2 * SSM_INNER) // bc_w
    dt_col = (2 * SSM_INNER + bc_w) // DT_LANES
    pad16 = lambda a: jnp.pad(a.astype(f32), (0, LANES - SSM_HEADS)).reshape(1, LANES)
    per_lane = lambda a: jnp.repeat(a.astype(f32), SSM_HEAD_DIM).reshape(1, SSM_INNER)
    wx, wb = p['conv_w'][:, :SSM_INNER].astype(f32), p['conv_w'][:, SSM_INNER:].astype(f32)
    bx, bb = p['conv_b'][:SSM_INNER].reshape(1, -1).astype(f32), p['conv_b'][SSM_INNER:].reshape(1, -1).astype(f32)
    skip = per_lane(p['d_skip'])
    nw = p['ssm_norm_w'].reshape(1, SSM_INNER).astype(f32)

    def run(rev, part):
        pos = (lambda c: nchunk - 1 - c) if rev else (lambda c: c)

        def chunk(w, colblk):
            return pl.BlockSpec((SSM_CHUNK, w), lambda b, c: (b * nchunk + pos(c), colblk))

        def halo(w, colblk, side):
            def index(b, c):
                r = (b * nchunk + pos(c)) * halo_per_chunk
                r = r - 1 if side < 0 else r + halo_per_chunk
                return (jnp.clip(r, 0, batch * nchunk * halo_per_chunk - 1), colblk)
            return pl.BlockSpec((HALO, w), index)

        const = lambda a: pl.BlockSpec(a.shape, lambda b, c: (0, 0))
        dtb = pad16(p['dt_bias_b'] if rev else p['dt_bias_f'])
        aneg = pad16(-jnp.exp((p['a_log_b'] if rev else p['a_log_f']).astype(f32)))
        consts = (wx, bx, wb, bb, dtb, aneg, skip, nw)
        return pl.pallas_call(
            functools.partial(_ssd_kernel, rev=rev),
            grid=(batch, nchunk),
            in_specs=[halo(SSM_INNER, xcol, -1), chunk(SSM_INNER, xcol), halo(SSM_INNER, xcol, 1),
                      halo(bc_w, bc_col, -1), chunk(bc_w, bc_col), halo(bc_w, bc_col, 1),
                      chunk(DT_LANES, dt_col), chunk(SSM_INNER, 0), chunk(SSM_INNER, 0)] + [const(a) for a in consts],
            out_specs=chunk(SSM_INNER, 0),
            out_shape=jax.ShapeDtypeStruct((batch * seq, SSM_INNER), f32),
            scratch_shapes=[pltpu.VMEM((SSM_HEADS // 2, SSM_STATE, LANES), f32)],
            compiler_params=pltpu.CompilerParams(dimension_semantics=("parallel", "arbitrary"),
                                                 vmem_limit_bytes=VMEM_LIMIT_BYTES),
            name="ssd_bwd" if rev else "ssd_fwd",
        )(proj, proj, proj, proj, proj, proj, proj, proj, part, *consts)

    return run(True, run(False, proj))


ROUTE_TM = 512
CAND_GROUPS = ((0, 16), (1, 8), (2, 8), (3, 8), (4, 8), (5, 8), (6, 8), (7, 8))
CAND_ROWS = sum(n for _, n in CAND_GROUPS) + SUBLANES
NO_PAIR = PEER_TOPK * PEER_TOPK
TOPK_SHIFT = PEER_TOPK.bit_length() - 1
assert 1 << TOPK_SHIFT == PEER_TOPK


def _cand_flat_ids():
    flat = []
    for a, n in CAND_GROUPS:
        flat += [a * PEER_TOPK + b if (a + 1) * (b + 1) <= PEER_TOPK else NO_PAIR for b in range(n)]
    flat += [a * PEER_TOPK for a in range(SUBLANES, PEER_TOPK)]
    return jnp.broadcast_to(jnp.array(flat, jnp.int32)[:, None], (CAND_ROWS, LANES))


def _top_rows(s, row_ids, k, sentinel):
    slot = lax.broadcasted_iota(jnp.int32, (k, LANES), 0)
    vals = jnp.zeros((k, LANES), jnp.float32)
    ids = jnp.zeros((k, LANES), jnp.int32)
    for r in range(k):
        m = jnp.max(s, axis=0, keepdims=True)
        i = jnp.min(jnp.where(s == m, row_ids, sentinel), axis=0, keepdims=True)
        vals = jnp.where(slot == r, m, vals)
        ids = jnp.where(slot == r, i, ids)
        s = jnp.where(row_ids == i, -jnp.inf, s)
    return vals, ids


def _lookup_rows(table, pos):
    slot = lax.broadcasted_iota(jnp.int32, table.shape, 0)
    return jnp.sum(jnp.where(slot == pos, table, 0), axis=0, keepdims=True)


def _peer_route_kernel(x_ref, g_ref, wq_ref, k1_ref, k2_ref, flat_ref, xn_ref, idx_ref, gate_ref, xb_ref):
    nt = (((1,), (1,)), ((), ()))

    @pl.when(pl.program_id(1) == 0)
    def _():
        x = x_ref[...]
        xn = x * lax.rsqrt(jnp.mean(x * x, axis=-1, keepdims=True) + EPS) * g_ref[...]
        xn_ref[...] = xn
        xb_ref[...] = xn.astype(jnp.bfloat16)

    q = jnp.dot(xb_ref[...], wq_ref[...], preferred_element_type=jnp.float32).astype(jnp.bfloat16)
    half = PEER_QDIM // 2
    s1 = lax.dot_general(k1_ref[...], q[:, :half], nt, preferred_element_type=jnp.float32)
    s2 = lax.dot_general(k2_ref[...], q[:, half:], nt, preferred_element_type=jnp.float32)
    key_ids = lax.broadcasted_iota(jnp.int32, (PEER_N_KEYS, LANES), 0)
    flat = flat_ref[...]
    for lt in range(ROUTE_TM // LANES):
        lanes = slice(lt * LANES, (lt + 1) * LANES)
        v1, i1 = _top_rows(s1[:, lanes], key_ids, PEER_TOPK, PEER_N_KEYS)
        v2, i2 = _top_rows(s2[:, lanes], key_ids, PEER_TOPK, PEER_N_KEYS)
        blocks = [v1[a:a + 1] + v2[:n] for a, n in CAND_GROUPS] + [v1[SUBLANES:] + v2[0:1]]
        cand = jnp.where(flat < NO_PAIR, jnp.concatenate(blocks, axis=0), -jnp.inf)
        top_s, top_flat = _top_rows(cand, flat, PEER_TOPK, 2 * NO_PAIR)
        ids = jnp.concatenate(
            [_lookup_rows(i1, top_flat[r:r + 1] >> TOPK_SHIFT) * PEER_N_KEYS + _lookup_rows(i2, top_flat[r:r + 1] & (PEER_TOPK - 1))
             for r in range(PEER_TOPK)], axis=0)
        e = jnp.exp(top_s - top_s[0:1])
        idx_ref[:, lanes] = ids * PACK_ROWS
        gate_ref[:, lanes] = e / jnp.sum(e, axis=0, keepdims=True)


def _peer_route(x, gain, wq, k1, k2):
    t = x.shape[0]
    assert t % ROUTE_TM == 0
    xn, idx_t, gate_t = pl.pallas_call(
        _peer_route_kernel,
        grid=(t // ROUTE_TM, PEER_HEADS),
        in_specs=[pl.BlockSpec((ROUTE_TM, D_MODEL), lambda i, h: (i, 0)),
                  pl.BlockSpec((1, D_MODEL), lambda i, h: (0, 0)),
                  pl.BlockSpec((D_MODEL, PEER_QDIM), lambda i, h: (0, h)),
                  pl.BlockSpec((PEER_N_KEYS, PEER_QDIM // 2), lambda i, h: (0, 0)),
                  pl.BlockSpec((PEER_N_KEYS, PEER_QDIM // 2), lambda i, h: (0, 0)),
                  pl.BlockSpec((CAND_ROWS, LANES), lambda i, h: (0, 0))],
        out_specs=[pl.BlockSpec((ROUTE_TM, D_MODEL), lambda i, h: (i, 0)),
                   pl.BlockSpec((PEER_TOPK, ROUTE_TM), lambda i, h: (h, i)),
                   pl.BlockSpec((PEER_TOPK, ROUTE_TM), lambda i, h: (h, i))],
        out_shape=[jax.ShapeDtypeStruct((t, D_MODEL), jnp.float32),
                   jax.ShapeDtypeStruct((PEER_PICKS, t), jnp.int32),
                   jax.ShapeDtypeStruct((PEER_PICKS, t), jnp.float32)],
        scratch_shapes=[pltpu.VMEM((ROUTE_TM, D_MODEL), jnp.bfloat16)],
        compiler_params=pltpu.CompilerParams(dimension_semantics=("parallel", "arbitrary"),
                                             vmem_limit_bytes=VMEM_LIMIT_BYTES),
        name="peer_route",
    )(x, gain.reshape(1, D_MODEL), wq.astype(jnp.bfloat16), k1.astype(jnp.bfloat16), k2.astype(jnp.bfloat16),
      _cand_flat_ids())
    return xn, idx_t.T, gate_t.T


def _encoder_layer(x, p):
    b, l, _ = x.shape
    t = b * l
    xt = x.reshape(t, D_MODEL)
    w_in, gain = p['w_in'], p['norm1_w']
    ssm_w, swa_w, dil_w = sum(IN_SPLITS[:6]), sum(IN_SPLITS[6:9]), sum(IN_SPLITS[9:12])
    ssm_proj = _matmul(xt, w_in[:, :ssm_w], gain=gain, tn=896, keep_pad=True)
    swa_proj = _matmul(xt, w_in[:, ssm_w:ssm_w + swa_w], gain=gain, tn=768)
    dil_proj = _matmul(xt, w_in[:, ssm_w + swa_w:ssm_w + swa_w + dil_w], gain=gain, tn=768)
    gate_logits = _matmul(xt, w_in[:, ssm_w + swa_w + dil_w:], gain=gain, tn=1024)
    y_a = _ssd_mixer(ssm_proj, p, b, l)
    cos, sin = _rope_tables(l)
    y_b = _window_attention(swa_proj, cos, sin, p['q_norm_b'], p['k_norm_b'], p['sink_b'], b, l)
    y_c = _dilated_attention(dil_proj, cos, sin, p['q_norm_c'], p['k_norm_c'], b, l)
    xt = _mixer_out(xt, y_a, y_b, y_c, gate_logits, p['w_a'], p['w_b'], p['w_c'], p['w_out'])
    xn, idx, gate = _peer_route(xt, p['norm2_w'], p['peer_wq'], p['peer_k1'], p['peer_k2'])
    xt = _peer_experts(xt, xn, idx, gate, p['u_packed'], p['v_packed'])
    return xt.reshape(b, l, D_MODEL)


_LAYER_KEYS = ('norm1_w', 'w_in', 'conv_w', 'conv_b', 'dt_bias_f', 'dt_bias_b', 'a_log_f', 'a_log_b', 'd_skip',
               'ssm_norm_w', 'w_a', 'q_norm_b', 'k_norm_b', 'sink_b', 'w_b', 'q_norm_c', 'k_norm_c', 'w_c', 'w_out',
               'norm2_w', 'peer_wq', 'peer_k1', 'peer_k2', 'peer_u', 'peer_v')


def kernel(x_prompt, x_sample, norm1_w, w_in, conv_w, conv_b, dt_bias_f, dt_bias_b, a_log_f, a_log_b, d_skip, ssm_norm_w, w_a, q_norm_b, k_norm_b, sink_b, w_b, q_norm_c, k_norm_c, w_c, w_out, norm2_w, peer_wq, peer_k1, peer_k2, peer_u, peer_v):
    stacked = dict(zip(_LAYER_KEYS, (norm1_w, w_in, conv_w, conv_b, dt_bias_f, dt_bias_b, a_log_f, a_log_b, d_skip,
                                     ssm_norm_w, w_a, q_norm_b, k_norm_b, sink_b, w_b, q_norm_c, k_norm_c, w_c, w_out,
                                     norm2_w, peer_wq, peer_k1, peer_k2, peer_u, peer_v)))
    layers = []
    for i in range(w_in.shape[0]):
        p = {k: v[i] for k, v in stacked.items()}
        p['u_packed'] = _pack_table(p['peer_u'])
        p['v_packed'] = _pack_table(p['peer_v'])
        layers.append(p)

    def trunk(x):
        for p in layers:
            x = _encoder_layer(x, p)
        return x

    return trunk(x_prompt), trunk(x_sample)
```
